```python
import math
import jax, jax.numpy as jnp
from jax import lax
import numpy as np

D_MODEL = 1024
BATCH = 16
SEQ = 2048
DEPTH = 2

SSD_WIDTH = D_MODEL
SSD_HEAD_DIM = 64
SSD_HEADS = SSD_WIDTH // SSD_HEAD_DIM
SSD_GROUPS = 2
SSD_STATE = 128
SSD_CONV = 4
SSD_CHUNK = 128
SSD_CONV_DIM = SSD_WIDTH + 2 * SSD_GROUPS * SSD_STATE
ATTN_HEAD_DIM = 64
ATTN_WIDTH = D_MODEL // 2
ATTN_Q_HEADS = ATTN_WIDTH // ATTN_HEAD_DIM
ATTN_KV_HEADS = 2
WINDOW = 128
CONF_WIDTH = D_MODEL // 2
CONF_KERNEL = 31
MIX_WIDTH = SSD_WIDTH + ATTN_WIDTH + CONF_WIDTH
IN_SPLITS = (MIX_WIDTH, SSD_CONV_DIM, SSD_HEADS, ATTN_Q_HEADS * ATTN_HEAD_DIM,
             ATTN_KV_HEADS * ATTN_HEAD_DIM, ATTN_KV_HEADS * ATTN_HEAD_DIM, 2 * CONF_WIDTH)
D_IN_PROJ = sum(IN_SPLITS)
EPS = 1e-5

kernel_name = "hybrid_ssd_swa_conformer_parallel_heads"


def _split(a, sizes):
    idx = np.cumsum(sizes)[:-1].tolist()
    return jnp.split(a, idx, axis=-1)


def rmsnorm(x, w):
    xf = x.astype(jnp.float32)
    y = xf * lax.rsqrt(jnp.mean(xf * xf, axis=-1, keepdims=True) + EPS)
    return (y * w.astype(jnp.float32)).astype(x.dtype)


def gated_group_rmsnorm(y, z, w, groups):
    g = (y * jax.nn.silu(z)).astype(jnp.float32)
    shp = g.shape
    g = g.reshape(shp[:-1] + (groups, shp[-1] // groups))
    g = g * lax.rsqrt(jnp.mean(g * g, axis=-1, keepdims=True) + EPS)
    return (g.reshape(shp) * w.astype(jnp.float32)).astype(y.dtype)


def layernorm(x, w, b):
    xf = x.astype(jnp.float32)
    mu = jnp.mean(xf, axis=-1, keepdims=True)
    xc = xf - mu
    y = xc * lax.rsqrt(jnp.mean(xc * xc, axis=-1, keepdims=True) + EPS)
    return (y * w.astype(jnp.float32) + b.astype(jnp.float32)).astype(x.dtype)


def causal_depthwise_conv(x, w, b):
    K, C = w.shape
    y = lax.conv_general_dilated(x, w[:, None, :].astype(x.dtype), window_strides=(1,),
                                 padding=[(K - 1, 0)], dimension_numbers=('NWC', 'WIO', 'NWC'),
                                 feature_group_count=C)
    return y + b.astype(x.dtype)


def ssd_chunked(x, dt, A, B, C):
    b, l, h, p = x.shape
    g, n = B.shape[2], B.shape[3]
    r = h // g
    nc = l // SSD_CHUNK
    Q = SSD_CHUNK
    x = x.reshape(b, nc, Q, g, r, p)
    dt = dt.reshape(b, nc, Q, g, r)
    B = B.reshape(b, nc, Q, g, n)
    C = C.reshape(b, nc, Q, g, n)
    a = dt * A.reshape(g, r)
    a_cs = jnp.cumsum(a, axis=2)
    xdt = x * dt[..., None]
    seg = a_cs[:, :, :, None] - a_cs[:, :, None, :]
    causal = jnp.tril(jnp.ones((Q, Q), dtype=bool))[None, None, :, :, None, None]
    L = jnp.exp(jnp.where(causal, seg, -jnp.inf))
    cb = jnp.einsum('bcign,bcjgn->bcijg', C, B)
    M = cb[..., None] * L
    y_diag = jnp.einsum('bcijgr,bcjgrp->bcigrp', M, xdt)
    decay_to_end = jnp.exp(a_cs[:, :, -1:] - a_cs)
    states = jnp.einsum('bcjgn,bcjgrp->bcgrpn', B, xdt * decay_to_end[..., None])
    chunk_decay = jnp.exp(a_cs[:, :, -1])

    def step(S, inp):
        st, dec = inp
        return dec[..., None, None] * S + st, S

    S0 = jnp.zeros((b, g, r, p, n), jnp.float32)
    _, prev = lax.scan(step, S0, (jnp.moveaxis(states, 1, 0), jnp.moveaxis(chunk_decay, 1, 0)))
    prev = jnp.moveaxis(prev, 0, 1)
    y_off = jnp.einsum('bcign,bcgrpn->bcigrp', C, prev) * jnp.exp(a_cs)[..., None]
    return (y_diag + y_off).reshape(b, l, h, p)


def swa_gqa_sinks(q, k, v, sinks):
    b, l, g, r, d = q.shape
    W = WINDOW
    nb = l // W
    qb = q.reshape(b, nb, W, g, r, d)
    kb = k.reshape(b, nb, W, g, d)
    vb = v.reshape(b, nb, W, g, d)
    pad = ((0, 0), (1, 0), (0, 0), (0, 0), (0, 0))
    kk = jnp.concatenate([jnp.pad(kb, pad)[:, :-1], kb], axis=2)
    vv = jnp.concatenate([jnp.pad(vb, pad)[:, :-1], vb], axis=2)
    s = jnp.einsum('bnqgrd,bnkgd->bngrqk', qb, kk,
                   preferred_element_type=jnp.float32) * (d ** -0.5)
    qi = jnp.arange(W)[:, None]
    kj = jnp.arange(2 * W)[None, :] - W
    rel = qi - kj
    band = (rel >= 0) & (rel < W)
    mask = band[None] & ((jnp.arange(nb)[:, None, None] > 0) | (kj[None] >= 0))
    s = jnp.where(mask[None, :, None, None], s, -jnp.inf)
    sk = sinks.astype(jnp.float32).reshape(g, r)[None, None, :, :, None, None]
    lse = jnp.logaddexp(jax.nn.logsumexp(s, axis=-1, keepdims=True), sk)
    pr = jnp.exp(s - lse)
    o = jnp.einsum('bngrqk,bnkgd->bnqgrd', pr.astype(v.dtype), vv)
    return o.reshape(b, l, g * r * d)


def hybrid_mixer(h, w_in, conv_w, conv_b, dt_bias, a_log, d_skip, ssd_norm_w, sinks,
                 dw_w, dw_b, ln_w, ln_b, w_out):
    b, l, _ = h.shape
    proj = h @ w_in
    z, xbc, dt, q, k, v, conf = _split(proj, IN_SPLITS)
    z_ssd, z_attn, z_conf = _split(z, (SSD_WIDTH, ATTN_WIDTH, CONF_WIDTH))
    xbc = jax.nn.silu(causal_depthwise_conv(xbc, conv_w, conv_b))
    xs, Bs, Cs = _split(xbc, (SSD_WIDTH, SSD_GROUPS * SSD_STATE, SSD_GROUPS * SSD_STATE))
    xs = xs.reshape(b, l, SSD_HEADS, SSD_HEAD_DIM).astype(jnp.float32)
    Bs = Bs.reshape(b, l, SSD_GROUPS, SSD_STATE).astype(jnp.float32)
    Cs = Cs.reshape(b, l, SSD_GROUPS, SSD_STATE).astype(jnp.float32)
    dtp = jax.nn.softplus(dt.astype(jnp.float32) + dt_bias.astype(jnp.float32))
    A = -jnp.exp(a_log.astype(jnp.float32))
    y = ssd_chunked(xs, dtp, A, Bs, Cs) + d_skip.astype(jnp.float32)[:, None] * xs
    y_ssd = gated_group_rmsnorm(y.reshape(b, l, SSD_WIDTH).astype(h.dtype), z_ssd,
                                ssd_norm_w, SSD_GROUPS)
    rep = ATTN_Q_HEADS // ATTN_KV_HEADS
    qh = q.reshape(b, l, ATTN_KV_HEADS, rep, ATTN_HEAD_DIM)
    kh = k.reshape(b, l, ATTN_KV_HEADS, ATTN_HEAD_DIM)
    vh = v.reshape(b, l, ATTN_KV_HEADS, ATTN_HEAD_DIM)
    y_attn = swa_gqa_sinks(qh, kh, vh, sinks) * jax.nn.silu(z_attn)
    ca, cg = _split(conf, (CONF_WIDTH, CONF_WIDTH))
    c = ca * jax.nn.sigmoid(cg)
    c = causal_depthwise_conv(c, dw_w, dw_b)
    c = jax.nn.silu(layernorm(c, ln_w, ln_b))
    y_conf = c * jax.nn.silu(z_conf)
    return jnp.concatenate([y_ssd, y_attn, y_conf], axis=-1) @ w_out


def setup_inputs(seed: int = 0) -> dict:
    key = jax.random.key(seed)
    ks = jax.random.split(key, 20)
    f32 = jnp.float32
    nrm = lambda k, s, sc: jax.random.normal(k, s, f32) * sc
    x = jax.random.normal(ks[0], (BATCH, SEQ, D_MODEL), f32)
    norm_w = 1.0 + nrm(ks[1], (DEPTH, D_MODEL), 0.02)
    w_in = nrm(ks[2], (DEPTH, D_MODEL, D_IN_PROJ), D_MODEL ** -0.5)
    ssd_conv_w = nrm(ks[3], (DEPTH, SSD_CONV, SSD_CONV_DIM), SSD_CONV ** -0.5)
    ssd_conv_b = nrm(ks[4], (DEPTH, SSD_CONV_DIM), 0.02)
    dt0 = jnp.exp(jax.random.uniform(ks[5], (DEPTH, SSD_HEADS), f32,
                                     math.log(1e-3), math.log(1e-1)))
    ssd_dt_bias = dt0 + jnp.log(-jnp.expm1(-dt0))
    ssd_a_log = jnp.log(jax.random.uniform(ks[6], (DEPTH, SSD_HEADS), f32, 1.0, 16.0))
    ssd_d = 1.0 + nrm(ks[7], (DEPTH, SSD_HEADS), 0.1)
    ssd_norm_w = 1.0 + nrm(ks[8], (DEPTH, SSD_WIDTH), 0.02)
    attn_sinks = nrm(ks[9], (DEPTH, ATTN_Q_HEADS), 1.0)
    conf_dw_w = nrm(ks[10], (DEPTH, CONF_KERNEL, CONF_WIDTH), CONF_KERNEL ** -0.5)
    conf_dw_b = nrm(ks[11], (DEPTH, CONF_WIDTH), 0.02)
    conf_ln_w = 1.0 + nrm(ks[12], (DEPTH, CONF_WIDTH), 0.02)
    conf_ln_b = nrm(ks[13], (DEPTH, CONF_WIDTH), 0.02)
    w_out = nrm(ks[14], (DEPTH, MIX_WIDTH, D_MODEL), MIX_WIDTH ** -0.5)
    final_norm_w = 1.0 + nrm(ks[15], (D_MODEL,), 0.02)
    return {"x": x, "norm_w": norm_w, "w_in": w_in, "ssd_conv_w": ssd_conv_w,
            "ssd_conv_b": ssd_conv_b, "ssd_dt_bias": ssd_dt_bias, "ssd_a_log": ssd_a_log,
            "ssd_d": ssd_d, "ssd_norm_w": ssd_norm_w, "attn_sinks": attn_sinks,
            "conf_dw_w": conf_dw_w, "conf_dw_b": conf_dw_b, "conf_ln_w": conf_ln_w,
            "conf_ln_b": conf_ln_b, "w_out": w_out, "final_norm_w": final_norm_w}


def reference(x, norm_w, w_in, ssd_conv_w, ssd_conv_b, ssd_dt_bias, ssd_a_log, ssd_d,
              ssd_norm_w, attn_sinks, conf_dw_w, conf_dw_b, conf_ln_w, conf_ln_b, w_out,
              final_norm_w):
    for i in range(DEPTH):
        h = rmsnorm(x, norm_w[i])
        x = x + hybrid_mixer(h, w_in[i], ssd_conv_w[i], ssd_conv_b[i], ssd_dt_bias[i],
                             ssd_a_log[i], ssd_d[i], ssd_norm_w[i], attn_sinks[i],
                             conf_dw_w[i], conf_dw_b[i], conf_ln_w[i], conf_ln_b[i], w_out[i])
    return rmsnorm(x, final_norm_w)
```

```python
import functools

import jax
import jax.numpy as jnp
from jax import lax
from jax.experimental import pallas as pl
from jax.experimental.pallas import tpu as pltpu

F32 = jnp.float32
BF16 = jnp.bfloat16

D_MODEL = 1024
SSD_WIDTH = 1024
SSD_HEAD_DIM = 64
SSD_HEADS = 16
SSD_GROUPS = 2
SSD_STATE = 128
SSD_CONV = 4
SSD_CHUNK = 128
SSD_CONV_DIM = SSD_WIDTH + 2 * SSD_GROUPS * SSD_STATE
ATTN_HEAD_DIM = 64
ATTN_WIDTH = 512
ATTN_Q_HEADS = 8
ATTN_KV_HEADS = 2
WINDOW = 128
CONF_WIDTH = 512
CONF_KERNEL = 31
MIX_WIDTH = SSD_WIDTH + ATTN_WIDTH + CONF_WIDTH
EPS = 1e-5

LANES = 128
SUBLANES = 8
VMEM_LIMIT = 56 * 1024 * 1024

PROJ_ROWS = 512
CONF_ROWS = 256
CONF_HALO = 32
SSD_HALO = SUBLANES


def _sigmoid(v):
    return 1.0 / (1.0 + jnp.exp(-v))


def _silu(v):
    return v * _sigmoid(v)


def _dot(a, b):
    return jnp.dot(a, b, preferred_element_type=F32)


def _dot_nt(a, b):
    return lax.dot_general(a, b, (((1,), (1,)), ((), ())), preferred_element_type=F32)


def _in_proj_kernel(x_ref, nw_ref, wz_ref, wxbc_ref, wdt_ref, wqkv_ref, wconf_ref,
                    gz_ref, xbc_ref, dt_ref, qkv_ref, conf_ref):
    x = x_ref[...]
    h = x * lax.rsqrt(jnp.mean(x * x, axis=-1, keepdims=True) + EPS) * nw_ref[...]
    hb = h.astype(BF16)
    gz_ref[...] = _silu(_dot(hb, wz_ref[...])).astype(BF16)
    xbc_ref[...] = _dot(hb, wxbc_ref[...]).astype(BF16)
    dt_ref[...] = _dot(hb, wdt_ref[...])
    qkv_ref[...] = _dot(hb, wqkv_ref[...]).astype(BF16)
    conf_ref[...] = _dot(hb, wconf_ref[...]).astype(BF16)


def _in_proj(x2, nw, wz, wxbc, wdt, wqkv, wconf):
    t = x2.shape[0]
    tm = min(PROJ_ROWS, t)
    row = lambda n: pl.BlockSpec((tm, n), lambda i: (i, 0))
    full = lambda a: pl.BlockSpec(a.shape, lambda i: (0, 0))
    widths = (wz.shape[1], wxbc.shape[1], wdt.shape[1], wqkv.shape[1], wconf.shape[1])
    dtypes = (BF16, BF16, F32, BF16, BF16)
    return pl.pallas_call(
        _in_proj_kernel,
        grid=(t // tm,),
        in_specs=[row(D_MODEL), full(nw), full(wz), full(wxbc), full(wdt), full(wqkv), full(wconf)],
        out_specs=[row(n) for n in widths],
        out_shape=[jax.ShapeDtypeStruct((t, n), d) for n, d in zip(widths, dtypes)],
        compiler_params=pltpu.CompilerParams(dimension_semantics=("parallel",),
                                             vmem_limit_bytes=VMEM_LIMIT),
        name="in_proj",
    )(x2, nw, wz, wxbc, wdt, wqkv, wconf)


def _out_proj_kernel(x_ref, ys_ref, ya_ref, yc_ref, ws_ref, wa_ref, wc_ref, fw_ref, o_ref, *,
                     final_norm):
    acc = _dot(ys_ref[...], ws_ref[...]) + _dot(ya_ref[...], wa_ref[...]) + _dot(yc_ref[...], wc_ref[...])
    xn = x_ref[...] + acc
    if final_norm:
        xn = xn * lax.rsqrt(jnp.mean(xn * xn, axis=-1, keepdims=True) + EPS) * fw_ref[...]
    o_ref[...] = xn


def _out_proj(x2, ys, ya, yc, ws, wa, wc, fw, final_norm):
    t = x2.shape[0]
    tm = min(PROJ_ROWS, t)
    row = lambda n: pl.BlockSpec((tm, n), lambda i: (i, 0))
    full = lambda a: pl.BlockSpec(a.shape, lambda i: (0, 0))
    return pl.pallas_call(
        functools.partial(_out_proj_kernel, final_norm=final_norm),
        grid=(t // tm,),
        in_specs=[row(D_MODEL), row(SSD_WIDTH), row(ATTN_WIDTH), row(CONF_WIDTH),
                  full(ws), full(wa), full(wc), full(fw)],
        out_specs=row(D_MODEL),
        out_shape=jax.ShapeDtypeStruct((t, D_MODEL), F32),
        compiler_params=pltpu.CompilerParams(dimension_semantics=("parallel",),
                                             vmem_limit_bytes=VMEM_LIMIT),
        name="out_proj",
    )(x2, ys, ya, yc, ws, wa, wc, fw)


def _conf_kernel(conf_ref, gz_ref, dww_ref, dwb_ref, lnw_ref, lnb_ref, o_ref, pad_ref):
    tq = conf_ref.shape[0]

    @pl.when(pl.program_id(1) == 0)
    def _():
        pad_ref[0:CONF_HALO, :] = jnp.zeros((CONF_HALO, CONF_WIDTH), F32)

    c = conf_ref[...].astype(F32)
    pad_ref[CONF_HALO:CONF_HALO + tq, :] = c[:, :CONF_WIDTH] * _sigmoid(c[:, CONF_WIDTH:])
    acc = jnp.broadcast_to(dwb_ref[...], (tq, CONF_WIDTH))
    base = CONF_HALO - (CONF_KERNEL - 1)
    for k in range(CONF_KERNEL):
        acc = acc + dww_ref[k:k + 1, :] * pad_ref[base + k:base + k + tq, :]
    pad_ref[0:CONF_HALO, :] = pad_ref[tq:tq + CONF_HALO, :]
    mu = jnp.mean(acc, axis=-1, keepdims=True)
    xc = acc - mu
    y = xc * lax.rsqrt(jnp.mean(xc * xc, axis=-1, keepdims=True) + EPS)
    y = y * lnw_ref[...] + lnb_ref[...]
    o_ref[...] = (_silu(y) * gz_ref[...].astype(F32)).astype(BF16)


def _conformer(conf, gz, dww, dwb, lnw, lnb):
    b, l, _ = conf.shape
    tq = min(CONF_ROWS, l)
    full = lambda a: pl.BlockSpec(a.shape, lambda i, j: (0, 0))
    gz_block = (SSD_WIDTH + ATTN_WIDTH) // CONF_WIDTH
    return pl.pallas_call(
        _conf_kernel,
        grid=(b, l // tq),
        in_specs=[pl.BlockSpec((None, tq, 2 * CONF_WIDTH), lambda i, j: (i, j, 0)),
                  pl.BlockSpec((None, tq, CONF_WIDTH), lambda i, j: (i, j, gz_block)),
                  full(dww), full(dwb), full(lnw), full(lnb)],
        out_specs=pl.BlockSpec((None, tq, CONF_WIDTH), lambda i, j: (i, j, 0)),
        out_shape=jax.ShapeDtypeStruct((b, l, CONF_WIDTH), BF16),
        scratch_shapes=[pltpu.VMEM((CONF_HALO + tq, CONF_WIDTH), F32)],
        compiler_params=pltpu.CompilerParams(dimension_semantics=("parallel", "arbitrary"),
                                             vmem_limit_bytes=VMEM_LIMIT),
        name="conformer",
    )(conf, gz, dww, dwb, lnw, lnb)


def _attn_kernel(qkv_ref, gz_ref, sink_ref, o_ref, kprev_ref, vprev_ref):
    w = WINDOW
    blk = pl.program_id(1)

    @pl.when(blk == 0)
    def _():
        kprev_ref[...] = jnp.zeros(kprev_ref.shape, BF16)
        vprev_ref[...] = jnp.zeros(vprev_ref.shape, BF16)

    kdup = qkv_ref[:, ATTN_WIDTH:ATTN_WIDTH + 2 * LANES]
    vdup = qkv_ref[:, ATTN_WIDTH + 2 * LANES:ATTN_WIDTH + 4 * LANES]
    lane = lax.broadcasted_iota(jnp.int32, (2 * w, LANES), 1)
    lo = lane < ATTN_HEAD_DIM
    qi = lax.broadcasted_iota(jnp.int32, (2 * w, 2 * w), 0) % w
    kk = lax.broadcasted_iota(jnp.int32, (2 * w, 2 * w), 1)
    rel = qi - (kk - w)
    kmin = jnp.where(blk > 0, 0, w)
    band = (rel >= 0) & (rel < w) & (kk >= kmin)
    zero = jnp.zeros((), BF16)
    outs = []
    for g in range(ATTN_KV_HEADS):
        kk2 = jnp.concatenate([kprev_ref[:, g * LANES:(g + 1) * LANES], kdup[:, g * LANES:(g + 1) * LANES]], axis=0)
        vv2 = jnp.concatenate([vprev_ref[:, g * LANES:(g + 1) * LANES], vdup[:, g * LANES:(g + 1) * LANES]], axis=0)
        kbd = jnp.concatenate([jnp.where(lo, kk2, zero), jnp.where(lo, zero, kk2)], axis=0)
        vbd = jnp.concatenate([jnp.where(lo, vv2, zero), jnp.where(lo, zero, vv2)], axis=0)
        qg = qkv_ref[:, g * 2 * LANES:(g + 1) * 2 * LANES]
        q2 = jnp.concatenate([qg[:, :LANES], qg[:, LANES:]], axis=0)
        s = _dot_nt(q2, kbd) * (ATTN_HEAD_DIM ** -0.5)
        ps, inv = [], []
        for half in range(2):
            sh = jnp.where(band, s[:, half * 2 * w:(half + 1) * 2 * w], -jnp.inf)
            row = lax.broadcasted_iota(jnp.int32, (2 * w, 1), 0)
            h0 = 4 * g + half
            sk = jnp.where(row < w, sink_ref[h0:h0 + 1, 0:1], sink_ref[h0 + 2:h0 + 3, 0:1])
            m = jnp.maximum(jnp.max(sh, axis=-1, keepdims=True), sk)
            p = jnp.exp(sh - m)
            den = jnp.sum(p, axis=-1, keepdims=True) + jnp.exp(sk - m)
            ps.append(p.astype(BF16))
            inv.append(1.0 / den)
        o2 = _dot(jnp.concatenate(ps, axis=1), vbd)
        o2 = o2 * jnp.where(lo, inv[0], inv[1])
        outs.append(o2[:w])
        outs.append(o2[w:])
    o = jnp.concatenate(outs, axis=1)
    o_ref[...] = (o * gz_ref[...].astype(F32)).astype(BF16)
    kprev_ref[...] = kdup
    vprev_ref[...] = vdup


def _attention(qkv, gz, sinks):
    b, l, n = qkv.shape
    w = WINDOW
    gz_block = SSD_WIDTH // ATTN_WIDTH
    return pl.pallas_call(
        _attn_kernel,
        grid=(b, l // w),
        in_specs=[pl.BlockSpec((None, w, n), lambda i, j: (i, j, 0)),
                  pl.BlockSpec((None, w, ATTN_WIDTH), lambda i, j: (i, j, gz_block)),
                  pl.BlockSpec(sinks.shape, lambda i, j: (0, 0))],
        out_specs=pl.BlockSpec((None, w, ATTN_WIDTH), lambda i, j: (i, j, 0)),
        out_shape=jax.ShapeDtypeStruct((b, l, ATTN_WIDTH), BF16),
        scratch_shapes=[pltpu.VMEM((w, 2 * LANES), BF16), pltpu.VMEM((w, 2 * LANES), BF16)],
        compiler_params=pltpu.CompilerParams(dimension_semantics=("parallel", "arbitrary"),
                                             vmem_limit_bytes=VMEM_LIMIT),
        name="attention",
    )(qkv, gz, sinks)


def _ssd_kernel(xbc_ref, dt_ref, gz_ref, cw_ref, cb_ref, dtb_ref, alog_ref, dskip_ref, nw_ref,
                tri_ref, expand_ref, o_ref, pad_ref, state_ref):
    q = SSD_CHUNK
    gw = SSD_WIDTH // SSD_GROUPS

    @pl.when(pl.program_id(1) == 0)
    def _():
        pad_ref[0:SSD_HALO, :] = jnp.zeros((SSD_HALO, SSD_CONV_DIM), F32)
        state_ref[...] = jnp.zeros(state_ref.shape, F32)

    pad_ref[SSD_HALO:SSD_HALO + q, :] = xbc_ref[...].astype(F32)
    acc = jnp.broadcast_to(cb_ref[...], (q, SSD_CONV_DIM))
    base = SSD_HALO - (SSD_CONV - 1)
    for k in range(SSD_CONV):
        acc = acc + cw_ref[k:k + 1, :] * pad_ref[base + k:base + k + q, :]
    pad_ref[0:SSD_HALO, :] = pad_ref[q:q + SSD_HALO, :]
    xc = _silu(acc)
    xs = xc[:, :SSD_WIDTH]
    xs_b = xs.astype(BF16)

    dtv = dt_ref[...] + dtb_ref[...]
    dtp = jnp.maximum(dtv, 0.0) + jnp.log1p(jnp.exp(-jnp.abs(dtv)))
    a = dtp * (-jnp.exp(alog_ref[...]))
    a_hi = a.astype(BF16)
    r1 = a - a_hi.astype(F32)
    a_mid = r1.astype(BF16)
    a_lo = (r1 - a_mid.astype(F32)).astype(BF16)
    tri = tri_ref[...]
    acs = _dot(tri, a_hi) + _dot(tri, a_mid) + _dot(tri, a_lo)
    acs_t = acs.T
    dtp_t = dtp.T
    decay_in = jnp.exp(acs)
    w_end = dtp * jnp.exp(acs[q - 1:q, :] - acs)
    expand = expand_ref[...]
    w_end_x = _dot(w_end.astype(BF16), expand)
    decay_in_x = _dot(decay_in.astype(BF16), expand)
    xdd = (xs * w_end_x).astype(BF16)

    ii = lax.broadcasted_iota(jnp.int32, (q, q), 0)
    jj = lax.broadcasted_iota(jnp.int32, (q, q), 1)
    causal = ii >= jj
    lane = lax.broadcasted_iota(jnp.int32, (q, LANES), 1)
    lo = lane < SSD_HEAD_DIM
    zero = jnp.zeros((), BF16)
    heads_per_group = SSD_HEADS // SSD_GROUPS
    y_parts = []
    for g in range(SSD_GROUPS):
        bg = xc[:, SSD_WIDTH + g * SSD_STATE:SSD_WIDTH + (g + 1) * SSD_STATE].astype(BF16)
        cg = xc[:, SSD_WIDTH + (SSD_GROUPS + g) * SSD_STATE:SSD_WIDTH + (SSD_GROUPS + g + 1) * SSD_STATE].astype(BF16)
        cbm = _dot_nt(cg, bg)
        sg = state_ref[g]
        y_off = _dot(cg, sg.astype(BF16))
        upd = lax.dot_general(bg, xdd[:, g * gw:(g + 1) * gw], (((0,), (0,)), ((), ())),
                              preferred_element_type=F32)
        state_ref[g] = decay_in_x[q - 1:q, g * gw:(g + 1) * gw] * sg + upd
        for p in range(heads_per_group // 2):
            ms = []
            for hh in range(2):
                h = g * heads_per_group + 2 * p + hh
                seg = acs[:, h:h + 1] - acs_t[h:h + 1, :]
                lmat = jnp.exp(jnp.where(causal, seg, -jnp.inf))
                ms.append((cbm * lmat * dtp_t[h:h + 1, :]).astype(BF16))
            c0 = g * gw + p * LANES
            xp = xs_b[:, c0:c0 + LANES]
            rhs = jnp.concatenate([jnp.where(lo, xp, zero), jnp.where(lo, zero, xp)], axis=0)
            y_diag = _dot(jnp.concatenate(ms, axis=1), rhs)
            y_parts.append(y_diag + y_off[:, p * LANES:(p + 1) * LANES] * decay_in_x[:, c0:c0 + LANES])
    y = jnp.concatenate(y_parts, axis=1) + dskip_ref[...] * xs

    gated = y * gz_ref[...].astype(F32)
    outs = []
    for g in range(SSD_GROUPS):
        gg = gated[:, g * gw:(g + 1) * gw]
        outs.append(gg * lax.rsqrt(jnp.mean(gg * gg, axis=-1, keepdims=True) + EPS))
    o_ref[...] = (jnp.concatenate(outs, axis=1) * nw_ref[...]).astype(BF16)


def _ssd(xbc, dt, gz, cw, cb, dtb, alog, dskip, nw, tri, expand):
    b, l, _ = xbc.shape
    q = SSD_CHUNK
    full = lambda a: pl.BlockSpec(a.shape, lambda i, j: (0, 0))
    return pl.pallas_call(
        _ssd_kernel,
        grid=(b, l // q),
        in_specs=[pl.BlockSpec((None, q, SSD_CONV_DIM), lambda i, j: (i, j, 0)),
                  pl.BlockSpec((None, q, LANES), lambda i, j: (i, j, 0)),
                  pl.BlockSpec((None, q, SSD_WIDTH), lambda i, j: (i, j, 0)),
                  full(cw), full(cb), full(dtb), full(alog), full(dskip), full(nw), full(tri), full(expand)],
        out_specs=pl.BlockSpec((None, q, SSD_WIDTH), lambda i, j: (i, j, 0)),
        out_shape=jax.ShapeDtypeStruct((b, l, SSD_WIDTH), BF16),
        scratch_shapes=[pltpu.VMEM((SSD_HALO + q, SSD_CONV_DIM), F32),
                        pltpu.VMEM((SSD_GROUPS, SSD_STATE, SSD_WIDTH // SSD_GROUPS), F32)],
        compiler_params=pltpu.CompilerParams(dimension_semantics=("parallel", "arbitrary"),
                                             vmem_limit_bytes=VMEM_LIMIT),
        name="ssd",
    )(xbc, dt, gz, cw, cb, dtb, alog, dskip, nw, tri, expand)


def _pad_lanes(v, n=LANES):
    return jnp.pad(v.astype(F32), (0, n - v.shape[0]))[None, :]


def kernel(x, norm_w, w_in, ssd_conv_w, ssd_conv_b, ssd_dt_bias, ssd_a_log, ssd_d, ssd_norm_w, attn_sinks, conf_dw_w, conf_dw_b, conf_ln_w, conf_ln_b, w_out, final_norm_w):
    b, l, d = x.shape
    depth = w_in.shape[0]
    t = b * l
    kv = ATTN_KV_HEADS * ATTN_HEAD_DIM
    o_xbc = MIX_WIDTH
    o_dt = o_xbc + SSD_CONV_DIM
    o_q = o_dt + SSD_HEADS
    o_k = o_q + ATTN_WIDTH
    o_v = o_k + kv
    o_conf = o_v + kv

    q = SSD_CHUNK
    tri = (lax.broadcasted_iota(jnp.int32, (q, q), 0) >= lax.broadcasted_iota(jnp.int32, (q, q), 1)).astype(BF16)
    head_of_lane = lax.broadcasted_iota(jnp.int32, (LANES, SSD_WIDTH), 1) // SSD_HEAD_DIM
    expand = (lax.broadcasted_iota(jnp.int32, (LANES, SSD_WIDTH), 0) == head_of_lane).astype(BF16)

    def dup_heads(wm):
        wm = wm.reshape(d, ATTN_KV_HEADS, 1, ATTN_HEAD_DIM)
        return jnp.broadcast_to(wm, (d, ATTN_KV_HEADS, 2, ATTN_HEAD_DIM)).reshape(d, 2 * kv)

    x2 = x.reshape(t, d)
    for i in range(depth):
        wi = w_in[i]
        wz = wi[:, :o_xbc].astype(BF16)
        wxbc = wi[:, o_xbc:o_dt].astype(BF16)
        wdt = jnp.pad(wi[:, o_dt:o_q], ((0, 0), (0, LANES - SSD_HEADS))).astype(BF16)
        wqkv = jnp.concatenate([wi[:, o_q:o_k], dup_heads(wi[:, o_k:o_v]), dup_heads(wi[:, o_v:o_conf])],
                               axis=1).astype(BF16)
        wconf = wi[:, o_conf:].astype(BF16)
        gz, xbc, dt, qkv, conf = _in_proj(x2, norm_w[i][None, :], wz, wxbc, wdt, wqkv, wconf)

        y_ssd = _ssd(xbc.reshape(b, l, -1), dt.reshape(b, l, -1), gz.reshape(b, l, -1),
                     ssd_conv_w[i], ssd_conv_b[i][None, :], _pad_lanes(ssd_dt_bias[i]),
                     _pad_lanes(ssd_a_log[i]), jnp.repeat(ssd_d[i], SSD_HEAD_DIM)[None, :],
                     ssd_norm_w[i][None, :], tri, expand)
        sinks = jnp.broadcast_to(attn_sinks[i].astype(F32)[:, None], (ATTN_Q_HEADS, LANES))
        y_attn = _attention(qkv.reshape(b, l, -1), gz.reshape(b, l, -1), sinks)
        y_conf = _conformer(conf.reshape(b, l, -1), gz.reshape(b, l, -1), conf_dw_w[i], conf_dw_b[i][None, :],
                            conf_ln_w[i][None, :], conf_ln_b[i][None, :])

        wo = w_out[i].astype(BF16)
        x2 = _out_proj(x2, y_ssd.reshape(t, -1), y_attn.reshape(t, -1), y_conf.reshape(t, -1),
                       wo[:SSD_WIDTH], wo[SSD_WIDTH:SSD_WIDTH + ATTN_WIDTH], wo[SSD_WIDTH + ATTN_WIDTH:],
                       final_norm_w[None, :], final_norm=(i == depth - 1))
    return x2.reshape(b, l, d)
```

```python
import functools

import jax
import jax.numpy as jnp
from jax import lax
from jax.experimental import pallas as pl
from jax.experimental.pallas import tpu as pltpu

F32 = jnp.float32
BF16 = jnp.bfloat16

D_MODEL = 1024
SSD_WIDTH = 1024
SSD_HEAD_DIM = 64
SSD_HEADS = 16
SSD_GROUPS = 2
SSD_STATE = 128
SSD_CONV = 4
SSD_CHUNK = 128
SSD_CONV_DIM = SSD_WIDTH + 2 * SSD_GROUPS * SSD_STATE
ATTN_HEAD_DIM = 64
ATTN_WIDTH = 512
ATTN_Q_HEADS = 8
ATTN_KV_HEADS = 2
WINDOW = 128
CONF_WIDTH = 512
CONF_KERNEL = 31
MIX_WIDTH = SSD_WIDTH + ATTN_WIDTH + CONF_WIDTH
EPS = 1e-5

LANES = 128
SUBLANES = 8
VMEM_LIMIT = 56 * 1024 * 1024

PROJ_ROWS = 512
MIX_ROWS = 512
CONF_HALO = 32
SSD_HALO = SUBLANES


def _sigmoid(v):
    return 1.0 / (1.0 + jnp.exp(-v))


def _silu(v):
    return v * _sigmoid(v)


def _dot(a, b):
    return jnp.dot(a, b, preferred_element_type=F32)


def _dot_nt(a, b):
    return lax.dot_general(a, b, (((1,), (1,)), ((), ())), preferred_element_type=F32)


def _rows_ahead(v, o):
    return pltpu.roll(v, v.shape[0] - o, axis=0)


def _in_proj_kernel(x_ref, nw_ref, wz_ref, wxbc_ref, wdt_ref, wqkv_ref, wconf_ref,
                    cw_ref, cb_ref, dww_ref, dwb_ref, lnw_ref, lnb_ref,
                    gz_ref, xc_ref, dt_ref, qkv_ref, yconf_ref,
                    cpad_ref, cshift_ref, xpad_ref):
    tm = x_ref.shape[0]

    @pl.when(pl.program_id(1) == 0)
    def _():
        cpad_ref[0:CONF_HALO, :] = jnp.zeros((CONF_HALO, CONF_WIDTH), F32)
        xpad_ref[0:SSD_HALO, :] = jnp.zeros((SSD_HALO, SSD_CONV_DIM), F32)

    x = x_ref[...]
    h = x * lax.rsqrt(jnp.mean(x * x, axis=-1, keepdims=True) + EPS) * nw_ref[...]
    hb = h.astype(BF16)

    c = _dot(hb, wconf_ref[...])
    cpad_ref[CONF_HALO:CONF_HALO + tm, :] = c[:, :CONF_WIDTH] * _sigmoid(c[:, CONF_WIDTH:])
    padded = cpad_ref[...]
    n_shift = CONF_HALO + tm - SUBLANES
    for r in range(1, SUBLANES):
        cshift_ref[r - 1] = _rows_ahead(padded, r)[0:n_shift]
    acc = jnp.broadcast_to(dwb_ref[...], (tm, CONF_WIDTH))
    base = CONF_HALO - (CONF_KERNEL - 1)
    for k in range(CONF_KERNEL):
        a8, r = divmod(base + k, SUBLANES)
        if r == 0:
            tap = cpad_ref[a8 * SUBLANES:a8 * SUBLANES + tm, :]
        else:
            tap = cshift_ref[r - 1, a8 * SUBLANES:a8 * SUBLANES + tm, :]
        acc = acc + dww_ref[k:k + 1, :] * tap
    cpad_ref[0:CONF_HALO, :] = cpad_ref[tm:tm + CONF_HALO, :]
    mu = jnp.mean(acc, axis=-1, keepdims=True)
    cc = acc - mu
    ln = cc * lax.rsqrt(jnp.mean(cc * cc, axis=-1, keepdims=True) + EPS)
    ln = ln * lnw_ref[...] + lnb_ref[...]

    gz = _silu(_dot(hb, wz_ref[...]))
    gz_ref[...] = gz[:, :SSD_WIDTH + ATTN_WIDTH].astype(BF16)
    yconf_ref[...] = (_silu(ln) * gz[:, SSD_WIDTH + ATTN_WIDTH:]).astype(BF16)

    xbc = _dot(hb, wxbc_ref[...])
    xpad_ref[SSD_HALO:SSD_HALO + tm, :] = xbc
    xpadded = xpad_ref[...]
    acc2 = cb_ref[...] + cw_ref[SSD_CONV - 1:SSD_CONV, :] * xbc
    for k in range(SSD_CONV - 1):
        acc2 = acc2 + cw_ref[k:k + 1, :] * _rows_ahead(xpadded, SSD_HALO - (SSD_CONV - 1) + k)[0:tm]
    xpad_ref[0:SSD_HALO, :] = xpad_ref[tm:tm + SSD_HALO, :]
    xc_ref[...] = _silu(acc2).astype(BF16)

    dt_ref[...] = _dot(hb, wdt_ref[...])
    qkv_ref[...] = _dot(hb, wqkv_ref[...]).astype(BF16)


def _in_proj(x, nw, wz, wxbc, wdt, wqkv, wconf, cw, cb, dww, dwb, lnw, lnb):
    b, l, d = x.shape
    tm = min(PROJ_ROWS, l)
    row = lambda n: pl.BlockSpec((None, tm, n), lambda i, j: (i, j, 0))
    full = lambda a: pl.BlockSpec(a.shape, lambda i, j: (0, 0))
    consts = (nw, wz, wxbc, wdt, wqkv, wconf, cw, cb, dww, dwb, lnw, lnb)
    widths = (SSD_WIDTH + ATTN_WIDTH, SSD_CONV_DIM, wdt.shape[1], wqkv.shape[1], CONF_WIDTH)
    dtypes = (BF16, BF16, F32, BF16, BF16)
    return pl.pallas_call(
        _in_proj_kernel,
        grid=(b, l // tm),
        in_specs=[row(d)] + [full(a) for a in consts],
        out_specs=[row(n) for n in widths],
        out_shape=[jax.ShapeDtypeStruct((b, l, n), dt) for n, dt in zip(widths, dtypes)],
        scratch_shapes=[pltpu.VMEM((CONF_HALO + tm, CONF_WIDTH), F32),
                        pltpu.VMEM((SUBLANES - 1, CONF_HALO + tm - SUBLANES, CONF_WIDTH), F32),
                        pltpu.VMEM((SSD_HALO + tm, SSD_CONV_DIM), F32)],
        compiler_params=pltpu.CompilerParams(dimension_semantics=("parallel", "arbitrary"),
                                             vmem_limit_bytes=VMEM_LIMIT),
        name="in_proj",
    )(x, *consts)


def _out_proj_kernel(x_ref, ys_ref, ya_ref, yc_ref, ws_ref, wa_ref, wc_ref, fw_ref, o_ref, *,
                     final_norm):
    acc = _dot(ys_ref[...], ws_ref[...]) + _dot(ya_ref[...], wa_ref[...]) + _dot(yc_ref[...], wc_ref[...])
    xn = x_ref[...] + acc
    if final_norm:
        xn = xn * lax.rsqrt(jnp.mean(xn * xn, axis=-1, keepdims=True) + EPS) * fw_ref[...]
    o_ref[...] = xn


def _out_proj(x2, ys, ya, yc, ws, wa, wc, fw, final_norm):
    t = x2.shape[0]
    tm = min(PROJ_ROWS, t)
    row = lambda n: pl.BlockSpec((tm, n), lambda i: (i, 0))
    full = lambda a: pl.BlockSpec(a.shape, lambda i: (0, 0))
    return pl.pallas_call(
        functools.partial(_out_proj_kernel, final_norm=final_norm),
        grid=(t // tm,),
        in_specs=[row(D_MODEL), row(SSD_WIDTH), row(ATTN_WIDTH), row(CONF_WIDTH),
                  full(ws), full(wa), full(wc), full(fw)],
        out_specs=row(D_MODEL),
        out_shape=jax.ShapeDtypeStruct((t, D_MODEL), F32),
        compiler_params=pltpu.CompilerParams(dimension_semantics=("parallel",),
                                             vmem_limit_bytes=VMEM_LIMIT),
        name="out_proj",
    )(x2, ys, ya, yc, ws, wa, wc, fw)


def _attn_block(q, kprev, vprev, kcur, vcur, gz, sink_ref, kmin):
    w = WINDOW
    lane = lax.broadcasted_iota(jnp.int32, (2 * w, LANES), 1)
    lo = lane < ATTN_HEAD_DIM
    qi = lax.broadcasted_iota(jnp.int32, (2 * w, 2 * w), 0) % w
    kk = lax.broadcasted_iota(jnp.int32, (2 * w, 2 * w), 1)
    rel = qi - (kk - w)
    band = (rel >= 0) & (rel < w) & (kk >= kmin)
    row = lax.broadcasted_iota(jnp.int32, (2 * w, 1), 0)
    zero = jnp.zeros((), BF16)
    outs = []
    for g in range(ATTN_KV_HEADS):
        kk2 = jnp.concatenate([kprev[:, g * LANES:(g + 1) * LANES], kcur[:, g * LANES:(g + 1) * LANES]], axis=0)
        vv2 = jnp.concatenate([vprev[:, g * LANES:(g + 1) * LANES], vcur[:, g * LANES:(g + 1) * LANES]], axis=0)
        kbd = jnp.concatenate([jnp.where(lo, kk2, zero), jnp.where(lo, zero, kk2)], axis=0)
        vbd = jnp.concatenate([jnp.where(lo, vv2, zero), jnp.where(lo, zero, vv2)], axis=0)
        qg = q[:, g * 2 * LANES:(g + 1) * 2 * LANES]
        q2 = jnp.concatenate([qg[:, :LANES], qg[:, LANES:]], axis=0)
        s = _dot_nt(q2, kbd) * (ATTN_HEAD_DIM ** -0.5)
        ps, inv = [], []
        for half in range(2):
            sh = jnp.where(band, s[:, half * 2 * w:(half + 1) * 2 * w], -jnp.inf)
            h0 = 4 * g + half
            sk = jnp.where(row < w, sink_ref[h0:h0 + 1, 0:1], sink_ref[h0 + 2:h0 + 3, 0:1])
            m = jnp.maximum(jnp.max(sh, axis=-1, keepdims=True), sk)
            p = jnp.exp(sh - m)
            den = jnp.sum(p, axis=-1, keepdims=True) + jnp.exp(sk - m)
            ps.append(p.astype(BF16))
            inv.append(1.0 / den)
        o2 = _dot(jnp.concatenate(ps, axis=1), vbd)
        o2 = o2 * jnp.where(lo, inv[0], inv[1])
        outs.append(o2[:w])
        outs.append(o2[w:])
    o = jnp.concatenate(outs, axis=1)
    return (o * gz.astype(F32)).astype(BF16)


def _attn_kernel(qkv_ref, gz_ref, sink_ref, o_ref, kprev_ref, vprev_ref):
    w = WINDOW
    nblk = qkv_ref.shape[0] // w
    k0, v0 = ATTN_WIDTH, ATTN_WIDTH + 2 * LANES

    @pl.when(pl.program_id(1) == 0)
    def _():
        kprev_ref[...] = jnp.zeros(kprev_ref.shape, BF16)
        vprev_ref[...] = jnp.zeros(vprev_ref.shape, BF16)

    for c in range(nblk):
        rows = slice(c * w, (c + 1) * w)
        if c == 0:
            kprev, vprev = kprev_ref, vprev_ref
            kmin = jnp.where(pl.program_id(1) > 0, 0, w)
        else:
            prev = slice((c - 1) * w, c * w)
            kprev, vprev = qkv_ref[prev, k0:k0 + 2 * LANES], qkv_ref[prev, v0:v0 + 2 * LANES]
            kmin = 0
        o_ref[rows, :] = _attn_block(qkv_ref[rows, 0:ATTN_WIDTH], kprev, vprev,
                                     qkv_ref[rows, k0:k0 + 2 * LANES], qkv_ref[rows, v0:v0 + 2 * LANES],
                                     gz_ref[rows, :], sink_ref, kmin)
    last = slice((nblk - 1) * w, nblk * w)
    kprev_ref[...] = qkv_ref[last, k0:k0 + 2 * LANES]
    vprev_ref[...] = qkv_ref[last, v0:v0 + 2 * LANES]


def _attention(qkv, gz, sinks):
    b, l, n = qkv.shape
    tq = min(MIX_ROWS, l)
    gz_block = SSD_WIDTH // ATTN_WIDTH
    return pl.pallas_call(
        _attn_kernel,
        grid=(b, l // tq),
        in_specs=[pl.BlockSpec((None, tq, n), lambda i, j: (i, j, 0)),
                  pl.BlockSpec((None, tq, ATTN_WIDTH), lambda i, j: (i, j, gz_block)),
                  pl.BlockSpec(sinks.shape, lambda i, j: (0, 0))],
        out_specs=pl.BlockSpec((None, tq, ATTN_WIDTH), lambda i, j: (i, j, 0)),
        out_shape=jax.ShapeDtypeStruct((b, l, ATTN_WIDTH), BF16),
        scratch_shapes=[pltpu.VMEM((WINDOW, 2 * LANES), BF16), pltpu.VMEM((WINDOW, 2 * LANES), BF16)],
        compiler_params=pltpu.CompilerParams(dimension_semantics=("parallel", "arbitrary"),
                                             vmem_limit_bytes=VMEM_LIMIT),
        name="attention",
    )(qkv, gz, sinks)


def _ssd_chunk(xc, dt, gz, dtb, neg_a, dskip, nw, tri, expand, state_ref):
    q = SSD_CHUNK
    gw = SSD_WIDTH // SSD_GROUPS
    xs_b = xc[:, :SSD_WIDTH]
    xs = xs_b.astype(F32)

    dtv = dt + dtb
    dtp = jnp.maximum(dtv, 0.0) + jnp.log1p(jnp.exp(-jnp.abs(dtv)))
    a = dtp * neg_a
    a_hi = a.astype(BF16)
    r1 = a - a_hi.astype(F32)
    a_mid = r1.astype(BF16)
    a_lo = (r1 - a_mid.astype(F32)).astype(BF16)
    acs = _dot(tri, a_hi) + _dot(tri, a_mid) + _dot(tri, a_lo)
    acs_t = acs.T
    dtp_t = dtp.T
    decay_in = jnp.exp(acs)
    w_end = dtp * jnp.exp(acs[q - 1:q, :] - acs)
    w_end_x = _dot(w_end.astype(BF16), expand)
    decay_in_x = _dot(decay_in.astype(BF16), expand)
    xdd = (xs * w_end_x).astype(BF16)

    ii = lax.broadcasted_iota(jnp.int32, (q, q), 0)
    jj = lax.broadcasted_iota(jnp.int32, (q, q), 1)
    causal = ii >= jj
    lane = lax.broadcasted_iota(jnp.int32, (q, LANES), 1)
    lo = lane < SSD_HEAD_DIM
    zero = jnp.zeros((), BF16)
    heads_per_group = SSD_HEADS // SSD_GROUPS
    y_parts = []
    for g in range(SSD_GROUPS):
        bg = xc[:, SSD_WIDTH + g * SSD_STATE:SSD_WIDTH + (g + 1) * SSD_STATE]
        cg = xc[:, SSD_WIDTH + (SSD_GROUPS + g) * SSD_STATE:SSD_WIDTH + (SSD_GROUPS + g + 1) * SSD_STATE]
        cbm = _dot_nt(cg, bg)
        sg = state_ref[g]
        y_off = _dot(cg, sg.astype(BF16))
        upd = lax.dot_general(bg, xdd[:, g * gw:(g + 1) * gw], (((0,), (0,)), ((), ())),
                              preferred_element_type=F32)
        state_ref[g] = decay_in_x[q - 1:q, g * gw:(g + 1) * gw] * sg + upd
        for p in range(heads_per_group // 2):
            ms = []
            for hh in range(2):
                h = g * heads_per_group + 2 * p + hh
                seg = acs[:, h:h + 1] - acs_t[h:h + 1, :]
                lmat = jnp.exp(jnp.where(causal, seg, -jnp.inf))
                ms.append((cbm * lmat * dtp_t[h:h + 1, :]).astype(BF16))
            c0 = g * gw + p * LANES
            xp = xs_b[:, c0:c0 + LANES]
            rhs = jnp.concatenate([jnp.where(lo, xp, zero), jnp.where(lo, zero, xp)], axis=0)
            y_diag = _dot(jnp.concatenate(ms, axis=1), rhs)
            y_parts.append(y_diag + y_off[:, p * LANES:(p + 1) * LANES] * decay_in_x[:, c0:c0 + LANES])
    y = jnp.concatenate(y_parts, axis=1) + dskip * xs

    gated = y * gz.astype(F32)
    outs = []
    for g in range(SSD_GROUPS):
        gg = gated[:, g * gw:(g + 1) * gw]
        outs.append(gg * lax.rsqrt(jnp.mean(gg * gg, axis=-1, keepdims=True) + EPS))
    return (jnp.concatenate(outs, axis=1) * nw).astype(BF16)


def _ssd_kernel(xc_ref, dt_ref, gz_ref, dtb_ref, alog_ref, dskip_ref, nw_ref, tri_ref, expand_ref,
                o_ref, state_ref):
    q = SSD_CHUNK

    @pl.when(pl.program_id(1) == 0)
    def _():
        state_ref[...] = jnp.zeros(state_ref.shape, F32)

    neg_a = -jnp.exp(alog_ref[...])
    for c in range(xc_ref.shape[0] // q):
        rows = slice(c * q, (c + 1) * q)
        o_ref[rows, :] = _ssd_chunk(xc_ref[rows, :], dt_ref[rows, :], gz_ref[rows, :], dtb_ref[...], neg_a,
                                    dskip_ref[...], nw_ref[...], tri_ref[...], expand_ref[...], state_ref)


def _ssd(xc, dt, gz, dtb, alog, dskip, nw, tri, expand):
    b, l, _ = xc.shape
    tq = min(MIX_ROWS, l)
    full = lambda a: pl.BlockSpec(a.shape, lambda i, j: (0, 0))
    return pl.pallas_call(
        _ssd_kernel,
        grid=(b, l // tq),
        in_specs=[pl.BlockSpec((None, tq, SSD_CONV_DIM), lambda i, j: (i, j, 0)),
                  pl.BlockSpec((None, tq, LANES), lambda i, j: (i, j, 0)),
                  pl.BlockSpec((None, tq, SSD_WIDTH), lambda i, j: (i, j, 0)),
                  full(dtb), full(alog), full(dskip), full(nw), full(tri), full(expand)],
        out_specs=pl.BlockSpec((None, tq, SSD_WIDTH), lambda i, j: (i, j, 0)),
        out_shape=jax.ShapeDtypeStruct((b, l, SSD_WIDTH), BF16),
        scratch_shapes=[pltpu.VMEM((SSD_GROUPS, SSD_STATE, SSD_WIDTH // SSD_GROUPS), F32)],
        compiler_params=pltpu.CompilerParams(dimension_semantics=("parallel", "arbitrary"),
                                             vmem_limit_bytes=VMEM_LIMIT),
        name="ssd",
    )(xc, dt, gz, dtb, alog, dskip, nw, tri, expand)


def _pad_lanes(v, n=LANES):
    return jnp.pad(v.astype(F32), (0, n - v.shape[0]))[None, :]


def kernel(x, norm_w, w_in, ssd_conv_w, ssd_conv_b, ssd_dt_bias, ssd_a_log, ssd_d, ssd_norm_w, attn_sinks, conf_dw_w, conf_dw_b, conf_ln_w, conf_ln_b, w_out, final_norm_w):
    b, l, d = x.shape
    depth = w_in.shape[0]
    t = b * l
    kv = ATTN_KV_HEADS * ATTN_HEAD_DIM
    o_xbc = MIX_WIDTH
    o_dt = o_xbc + SSD_CONV_DIM
    o_q = o_dt + SSD_HEADS
    o_k = o_q + ATTN_WIDTH
    o_v = o_k + kv
    o_conf = o_v + kv

    q = SSD_CHUNK
    tri = (lax.broadcasted_iota(jnp.int32, (q, q), 0) >= lax.broadcasted_iota(jnp.int32, (q, q), 1)).astype(BF16)
    head_of_lane = lax.broadcasted_iota(jnp.int32, (LANES, SSD_WIDTH), 1) // SSD_HEAD_DIM
    expand = (lax.broadcasted_iota(jnp.int32, (LANES, SSD_WIDTH), 0) == head_of_lane).astype(BF16)

    def dup_heads(wm):
        wm = wm.reshape(d, ATTN_KV_HEADS, 1, ATTN_HEAD_DIM)
        return jnp.broadcast_to(wm, (d, ATTN_KV_HEADS, 2, ATTN_HEAD_DIM)).reshape(d, 2 * kv)

    for i in range(depth):
        wi = w_in[i]
        wz = wi[:, :o_xbc].astype(BF16)
        wxbc = wi[:, o_xbc:o_dt].astype(BF16)
        wdt = jnp.pad(wi[:, o_dt:o_q], ((0, 0), (0, LANES - SSD_HEADS))).astype(BF16)
        wqkv = jnp.concatenate([wi[:, o_q:o_k], dup_heads(wi[:, o_k:o_v]), dup_heads(wi[:, o_v:o_conf])],
                               axis=1).astype(BF16)
        wconf = wi[:, o_conf:].astype(BF16)
        gz, xc, dt, qkv, y_conf = _in_proj(
            x, norm_w[i][None, :], wz, wxbc, wdt, wqkv, wconf,
            ssd_conv_w[i], ssd_conv_b[i][None, :], conf_dw_w[i], conf_dw_b[i][None, :],
            conf_ln_w[i][None, :], conf_ln_b[i][None, :])

        y_ssd = _ssd(xc, dt, gz, _pad_lanes(ssd_dt_bias[i]), _pad_lanes(ssd_a_log[i]),
                     jnp.repeat(ssd_d[i], SSD_HEAD_DIM)[None, :], ssd_norm_w[i][None, :], tri, expand)
        sinks = jnp.broadcast_to(attn_sinks[i].astype(F32)[:, None], (ATTN_Q_HEADS, LANES))
        y_attn = _attention(qkv, gz, sinks)

        wo = w_out[i].astype(BF16)
        x = _out_proj(x.reshape(t, d), y_ssd.reshape(t, -1), y_attn.reshape(t, -1), y_conf.reshape(t, -1),
                      wo[:SSD_WIDTH], wo[SSD_WIDTH:SSD_WIDTH + ATTN_WIDTH], wo[SSD_WIDTH + ATTN_WIDTH:],
                      final_norm_w[None, :], final_norm=(i == depth - 1)).reshape(b, l, d)
    return x
```

```python
import functools

import jax
import jax.numpy as jnp
from jax import lax
from jax.experimental import pallas as pl
from jax.experimental.pallas import tpu as pltpu

F32 = jnp.float32
BF16 = jnp.bfloat16

D_MODEL = 1024
SSD_WIDTH = 1024
SSD_HEAD_DIM = 64
SSD_HEADS = 16
SSD_GROUPS = 2
SSD_STATE = 128
SSD_CONV = 4
SSD_CHUNK = 128
SSD_CONV_DIM = SSD_WIDTH + 2 * SSD_GROUPS * SSD_STATE
ATTN_HEAD_DIM = 64
ATTN_WIDTH = 512
ATTN_Q_HEADS = 8
ATTN_KV_HEADS = 2
WINDOW = 128
CONF_WIDTH = 512
CONF_KERNEL = 31
MIX_WIDTH = SSD_WIDTH + ATTN_WIDTH + CONF_WIDTH
EPS = 1e-5

LANES = 128
SUBLANES = 8
VMEM_LIMIT = 56 * 1024 * 1024

PROJ_ROWS = 512
MIX_ROWS = 512
CONF_HALO = 32
SSD_HALO = SUBLANES


def _sigmoid(v):
    return 1.0 / (1.0 + jnp.exp(-v))


def _silu(v):
    return v * _sigmoid(v)


def _dot(a, b):
    return jnp.dot(a, b, preferred_element_type=F32)


def _dot_nt(a, b):
    return lax.dot_general(a, b, (((1,), (1,)), ((), ())), preferred_element_type=F32)


def _rows_ahead(v, o):
    return pltpu.roll(v, v.shape[0] - o, axis=0)


def _in_proj_kernel(x_ref, nw_ref, wz_ref, wxbc_ref, wdt_ref, wqkv_ref, wconf_ref,
                    cw_ref, cb_ref, dww_ref, dwb_ref, lnw_ref, lnb_ref,
                    gz_ref, xc_ref, dt_ref, qkv_ref, yconf_ref,
                    cpad_ref, cshift_ref, xpad_ref):
    tm = x_ref.shape[0]

    @pl.when(pl.program_id(1) == 0)
    def _():
        cpad_ref[0:CONF_HALO, :] = jnp.zeros((CONF_HALO, CONF_WIDTH), F32)
        xpad_ref[0:SSD_HALO, :] = jnp.zeros((SSD_HALO, SSD_CONV_DIM), F32)

    x = x_ref[...]
    h = x * lax.rsqrt(jnp.mean(x * x, axis=-1, keepdims=True) + EPS) * nw_ref[...]
    hb = h.astype(BF16)

    c = _dot(hb, wconf_ref[...])
    cpad_ref[CONF_HALO:CONF_HALO + tm, :] = c[:, :CONF_WIDTH] * _sigmoid(c[:, CONF_WIDTH:])
    padded = cpad_ref[...]
    n_shift = CONF_HALO + tm - SUBLANES
    for r in range(1, SUBLANES):
        cshift_ref[r - 1] = _rows_ahead(padded, r)[0:n_shift]
    acc = jnp.broadcast_to(dwb_ref[...], (tm, CONF_WIDTH))
    base = CONF_HALO - (CONF_KERNEL - 1)
    for k in range(CONF_KERNEL):
        a8, r = divmod(base + k, SUBLANES)
        if r == 0:
            tap = cpad_ref[a8 * SUBLANES:a8 * SUBLANES + tm, :]
        else:
            tap = cshift_ref[r - 1, a8 * SUBLANES:a8 * SUBLANES + tm, :]
        acc = acc + dww_ref[k:k + 1, :] * tap
    cpad_ref[0:CONF_HALO, :] = cpad_ref[tm:tm + CONF_HALO, :]
    mu = jnp.mean(acc, axis=-1, keepdims=True)
    cc = acc - mu
    ln = cc * lax.rsqrt(jnp.mean(cc * cc, axis=-1, keepdims=True) + EPS)
    ln = ln * lnw_ref[...] + lnb_ref[...]

    gz = _silu(_dot(hb, wz_ref[...]))
    gz_ref[...] = gz[:, :SSD_WIDTH + ATTN_WIDTH].astype(BF16)
    yconf_ref[...] = (_silu(ln) * gz[:, SSD_WIDTH + ATTN_WIDTH:]).astype(BF16)

    xbc = _dot(hb, wxbc_ref[...])
    xpad_ref[SSD_HALO:SSD_HALO + tm, :] = xbc
    xpadded = xpad_ref[...]
    acc2 = cb_ref[...] + cw_ref[SSD_CONV - 1:SSD_CONV, :] * xbc
    for k in range(SSD_CONV - 1):
        acc2 = acc2 + cw_ref[k:k + 1, :] * _rows_ahead(xpadded, SSD_HALO - (SSD_CONV - 1) + k)[0:tm]
    xpad_ref[0:SSD_HALO, :] = xpad_ref[tm:tm + SSD_HALO, :]
    xc_ref[...] = _silu(acc2).astype(BF16)

    dt_ref[...] = _dot(hb, wdt_ref[...])
    qkv_ref[...] = _dot(hb, wqkv_ref[...]).astype(BF16)


def _in_proj(x, nw, wz, wxbc, wdt, wqkv, wconf, cw, cb, dww, dwb, lnw, lnb):
    b, l, d = x.shape
    tm = min(PROJ_ROWS, l)
    row = lambda n: pl.BlockSpec((None, tm, n), lambda i, j: (i, j, 0))
    full = lambda a: pl.BlockSpec(a.shape, lambda i, j: (0, 0))
    consts = (nw, wz, wxbc, wdt, wqkv, wconf, cw, cb, dww, dwb, lnw, lnb)
    widths = (SSD_WIDTH + ATTN_WIDTH, SSD_CONV_DIM, wdt.shape[1], wqkv.shape[1], CONF_WIDTH)
    dtypes = (BF16, BF16, F32, BF16, BF16)
    return pl.pallas_call(
        _in_proj_kernel,
        grid=(b, l // tm),
        in_specs=[row(d)] + [full(a) for a in consts],
        out_specs=[row(n) for n in widths],
        out_shape=[jax.ShapeDtypeStruct((b, l, n), dt) for n, dt in zip(widths, dtypes)],
        scratch_shapes=[pltpu.VMEM((CONF_HALO + tm, CONF_WIDTH), F32),
                        pltpu.VMEM((SUBLANES - 1, CONF_HALO + tm - SUBLANES, CONF_WIDTH), F32),
                        pltpu.VMEM((SSD_HALO + tm, SSD_CONV_DIM), F32)],
        compiler_params=pltpu.CompilerParams(dimension_semantics=("parallel", "arbitrary"),
                                             vmem_limit_bytes=VMEM_LIMIT),
        name="in_proj",
    )(x, *consts)


def _attn_block(q, kprev, vprev, kcur, vcur, gz, sink_ref, kmin):
    w = WINDOW
    lane = lax.broadcasted_iota(jnp.int32, (2 * w, LANES), 1)
    lo = lane < ATTN_HEAD_DIM
    qi = lax.broadcasted_iota(jnp.int32, (2 * w, 2 * w), 0) % w
    kk = lax.broadcasted_iota(jnp.int32, (2 * w, 2 * w), 1)
    rel = qi - (kk - w)
    band = (rel >= 0) & (rel < w) & (kk >= kmin)
    row = lax.broadcasted_iota(jnp.int32, (2 * w, 1), 0)
    zero = jnp.zeros((), BF16)
    lo_f = jnp.where(lo, 1.0, 0.0)
    ones_bd = jnp.concatenate([lo_f, 1.0 - lo_f], axis=0).astype(BF16)
    outs = []
    for g in range(ATTN_KV_HEADS):
        kk2 = jnp.concatenate([kprev[:, g * LANES:(g + 1) * LANES], kcur[:, g * LANES:(g + 1) * LANES]], axis=0)
        vv2 = jnp.concatenate([vprev[:, g * LANES:(g + 1) * LANES], vcur[:, g * LANES:(g + 1) * LANES]], axis=0)
        kbd = jnp.concatenate([jnp.where(lo, kk2, zero), jnp.where(lo, zero, kk2)], axis=0)
        vbd = jnp.concatenate([jnp.where(lo, vv2, zero), jnp.where(lo, zero, vv2)], axis=0)
        qg = q[:, g * 2 * LANES:(g + 1) * 2 * LANES]
        q2 = jnp.concatenate([qg[:, :LANES], qg[:, LANES:]], axis=0)
        s = _dot_nt(q2, kbd) * (ATTN_HEAD_DIM ** -0.5)
        ps, sink_terms = [], []
        for half in range(2):
            sh = jnp.where(band, s[:, half * 2 * w:(half + 1) * 2 * w], -jnp.inf)
            h0 = 4 * g + half
            sk = jnp.where(row < w, sink_ref[h0:h0 + 1, 0:1], sink_ref[h0 + 2:h0 + 3, 0:1])
            m = jnp.maximum(jnp.max(sh, axis=-1, keepdims=True), sk)
            ps.append(jnp.exp(sh - m).astype(BF16))
            sink_terms.append(jnp.exp(sk - m))
        o2 = _dot(jnp.concatenate(ps, axis=1), jnp.concatenate([vbd, ones_bd], axis=1))
        den = o2[:, LANES:] + jnp.where(lo, sink_terms[0], sink_terms[1])
        o2 = o2[:, :LANES] / den
        outs.append(o2[:w])
        outs.append(o2[w:])
    o = jnp.concatenate(outs, axis=1)
    return (o * gz.astype(F32)).astype(BF16)


def _ssd_chunk(xc, dt, gz, dtb, neg_a, dskip, nw, tri, expand, state_ref):
    q = SSD_CHUNK
    gw = SSD_WIDTH // SSD_GROUPS
    xs_b = xc[:, :SSD_WIDTH]
    xs = xs_b.astype(F32)

    dtv = dt + dtb
    dtp = jnp.maximum(dtv, 0.0) + jnp.log1p(jnp.exp(-jnp.abs(dtv)))
    a = dtp * neg_a
    a_hi = a.astype(BF16)
    r1 = a - a_hi.astype(F32)
    a_mid = r1.astype(BF16)
    a_lo = (r1 - a_mid.astype(F32)).astype(BF16)
    acs = _dot(tri, a_hi) + _dot(tri, a_mid) + _dot(tri, a_lo)
    acs_t = acs.T
    dtp_t = dtp.T
    decay_in = jnp.exp(acs)
    w_end = dtp * jnp.exp(acs[q - 1:q, :] - acs)
    w_end_x = _dot(w_end.astype(BF16), expand)
    decay_in_x = _dot(decay_in.astype(BF16), expand)
    xdd = (xs * w_end_x).astype(BF16)

    ii = lax.broadcasted_iota(jnp.int32, (q, q), 0)
    jj = lax.broadcasted_iota(jnp.int32, (q, q), 1)
    causal = ii >= jj
    lane = lax.broadcasted_iota(jnp.int32, (q, LANES), 1)
    lo = lane < SSD_HEAD_DIM
    zero = jnp.zeros((), BF16)
    heads_per_group = SSD_HEADS // SSD_GROUPS
    y_parts = []
    for g in range(SSD_GROUPS):
        bg = xc[:, SSD_WIDTH + g * SSD_STATE:SSD_WIDTH + (g + 1) * SSD_STATE]
        cg = xc[:, SSD_WIDTH + (SSD_GROUPS + g) * SSD_STATE:SSD_WIDTH + (SSD_GROUPS + g + 1) * SSD_STATE]
        cbm = _dot_nt(cg, bg)
        sg = state_ref[g]
        y_off = _dot(cg, sg.astype(BF16))
        upd = lax.dot_general(bg, xdd[:, g * gw:(g + 1) * gw], (((0,), (0,)), ((), ())),
                              preferred_element_type=F32)
        state_ref[g] = decay_in_x[q - 1:q, g * gw:(g + 1) * gw] * sg + upd
        for p in range(heads_per_group // 2):
            ms = []
            for hh in range(2):
                h = g * heads_per_group + 2 * p + hh
                seg = acs[:, h:h + 1] - acs_t[h:h + 1, :]
                lmat = jnp.exp(jnp.where(causal, seg, -jnp.inf))
                ms.append((cbm * lmat * dtp_t[h:h + 1, :]).astype(BF16))
            c0 = g * gw + p * LANES
            xp = xs_b[:, c0:c0 + LANES]
            rhs = jnp.concatenate([jnp.where(lo, xp, zero), jnp.where(lo, zero, xp)], axis=0)
            y_diag = _dot(jnp.concatenate(ms, axis=1), rhs)
            y_parts.append(y_diag + y_off[:, p * LANES:(p + 1) * LANES] * decay_in_x[:, c0:c0 + LANES])
    y = jnp.concatenate(y_parts, axis=1) + dskip * xs

    gated = y * gz.astype(F32)
    outs = []
    for g in range(SSD_GROUPS):
        gg = gated[:, g * gw:(g + 1) * gw]
        outs.append(gg * lax.rsqrt(jnp.mean(gg * gg, axis=-1, keepdims=True) + EPS))
    return (jnp.concatenate(outs, axis=1) * nw).astype(BF16)


def _mixer_kernel(x_ref, xc_ref, dt_ref, gz_ref, qkv_ref, yconf_ref, dtb_ref, alog_ref, dskip_ref, nw_ref,
                  tri_ref, expand_ref, sink_ref, wo_ref, fw_ref, o_ref,
                  state_ref, kprev_ref, vprev_ref, ys_ref, ya_ref, *, final_norm):
    q = SSD_CHUNK
    w = WINDOW
    k0, v0 = ATTN_WIDTH, ATTN_WIDTH + 2 * LANES
    step = pl.program_id(1)

    @pl.when(step == 0)
    def _():
        state_ref[...] = jnp.zeros(state_ref.shape, F32)
        kprev_ref[...] = jnp.zeros(kprev_ref.shape, BF16)
        vprev_ref[...] = jnp.zeros(vprev_ref.shape, BF16)

    neg_a = -jnp.exp(alog_ref[...])
    nblk = xc_ref.shape[0] // q
    for c in range(nblk):
        rows = slice(c * q, (c + 1) * q)
        ys_ref[rows, :] = _ssd_chunk(xc_ref[rows, :], dt_ref[rows, :], gz_ref[rows, 0:SSD_WIDTH], dtb_ref[...],
                                     neg_a, dskip_ref[...], nw_ref[...], tri_ref[...], expand_ref[...], state_ref)
        if c == 0:
            kprev, vprev = kprev_ref, vprev_ref
            kmin = jnp.where(step > 0, 0, w)
        else:
            prev = slice((c - 1) * w, c * w)
            kprev, vprev = qkv_ref[prev, k0:k0 + 2 * LANES], qkv_ref[prev, v0:v0 + 2 * LANES]
            kmin = 0
        ya_ref[rows, :] = _attn_block(qkv_ref[rows, 0:ATTN_WIDTH], kprev, vprev,
                                      qkv_ref[rows, k0:k0 + 2 * LANES], qkv_ref[rows, v0:v0 + 2 * LANES],
                                      gz_ref[rows, SSD_WIDTH:SSD_WIDTH + ATTN_WIDTH], sink_ref, kmin)
    last = slice((nblk - 1) * w, nblk * w)
    kprev_ref[...] = qkv_ref[last, k0:k0 + 2 * LANES]
    vprev_ref[...] = qkv_ref[last, v0:v0 + 2 * LANES]

    a0 = SSD_WIDTH + ATTN_WIDTH
    acc = (_dot(ys_ref[...], wo_ref[0:SSD_WIDTH, :]) + _dot(ya_ref[...], wo_ref[SSD_WIDTH:a0, :])
           + _dot(yconf_ref[...], wo_ref[a0:, :]))
    xn = x_ref[...] + acc
    if final_norm:
        xn = xn * lax.rsqrt(jnp.mean(xn * xn, axis=-1, keepdims=True) + EPS) * fw_ref[...]
    o_ref[...] = xn


def _mixer(x, xc, dt, gz, qkv, yconf, dtb, alog, dskip, nw, tri, expand, sinks, wo, fw, final_norm):
    b, l, d = x.shape
    tq = min(MIX_ROWS, l)
    row = lambda a: pl.BlockSpec((None, tq, a.shape[2]), lambda i, j: (i, j, 0))
    full = lambda a: pl.BlockSpec(a.shape, lambda i, j: (0, 0))
    rows_in = (x, xc, dt, gz, qkv, yconf)
    consts = (dtb, alog, dskip, nw, tri, expand, sinks, wo, fw)
    return pl.pallas_call(
        functools.partial(_mixer_kernel, final_norm=final_norm),
        grid=(b, l // tq),
        in_specs=[row(a) for a in rows_in] + [full(a) for a in consts],
        out_specs=pl.BlockSpec((None, tq, d), lambda i, j: (i, j, 0)),
        out_shape=jax.ShapeDtypeStruct((b, l, d), F32),
        scratch_shapes=[pltpu.VMEM((SSD_GROUPS, SSD_STATE, SSD_WIDTH // SSD_GROUPS), F32),
                        pltpu.VMEM((WINDOW, 2 * LANES), BF16),
                        pltpu.VMEM((WINDOW, 2 * LANES), BF16),
                        pltpu.VMEM((tq, SSD_WIDTH), BF16),
                        pltpu.VMEM((tq, ATTN_WIDTH), BF16)],
        compiler_params=pltpu.CompilerParams(dimension_semantics=("parallel", "arbitrary"),
                                             vmem_limit_bytes=VMEM_LIMIT),
        name="mixer",
    )(*rows_in, *consts)


def _pad_lanes(v, n=LANES):
    return jnp.pad(v.astype(F32), (0, n - v.shape[0]))[None, :]


def kernel(x, norm_w, w_in, ssd_conv_w, ssd_conv_b, ssd_dt_bias, ssd_a_log, ssd_d, ssd_norm_w, attn_sinks, conf_dw_w, conf_dw_b, conf_ln_w, conf_ln_b, w_out, final_norm_w):
    b, l, d = x.shape
    depth = w_in.shape[0]
    kv = ATTN_KV_HEADS * ATTN_HEAD_DIM
    o_xbc = MIX_WIDTH
    o_dt = o_xbc + SSD_CONV_DIM
    o_q = o_dt + SSD_HEADS
    o_k = o_q + ATTN_WIDTH
    o_v = o_k + kv
    o_conf = o_v + kv

    q = SSD_CHUNK
    tri = (lax.broadcasted_iota(jnp.int32, (q, q), 0) >= lax.broadcasted_iota(jnp.int32, (q, q), 1)).astype(BF16)
    head_of_lane = lax.broadcasted_iota(jnp.int32, (LANES, SSD_WIDTH), 1) // SSD_HEAD_DIM
    expand = (lax.broadcasted_iota(jnp.int32, (LANES, SSD_WIDTH), 0) == head_of_lane).astype(BF16)

    def dup_heads(wm):
        wm = wm.reshape(d, ATTN_KV_HEADS, 1, ATTN_HEAD_DIM)
        return jnp.broadcast_to(wm, (d, ATTN_KV_HEADS, 2, ATTN_HEAD_DIM)).reshape(d, 2 * kv)

    for i in range(depth):
        wi = w_in[i]
        wz = wi[:, :o_xbc].astype(BF16)
        wxbc = wi[:, o_xbc:o_dt].astype(BF16)
        wdt = jnp.pad(wi[:, o_dt:o_q], ((0, 0), (0, LANES - SSD_HEADS))).astype(BF16)
        wqkv = jnp.concatenate([wi[:, o_q:o_k], dup_heads(wi[:, o_k:o_v]), dup_heads(wi[:, o_v:o_conf])],
                               axis=1).astype(BF16)
        wconf = wi[:, o_conf:].astype(BF16)
        gz, xc, dt, qkv, y_conf = _in_proj(
            x, norm_w[i][None, :], wz, wxbc, wdt, wqkv, wconf,
            ssd_conv_w[i], ssd_conv_b[i][None, :], conf_dw_w[i], conf_dw_b[i][None, :],
            conf_ln_w[i][None, :], conf_ln_b[i][None, :])

        sinks = jnp.broadcast_to(attn_sinks[i].astype(F32)[:, None], (ATTN_Q_HEADS, LANES))
        x = _mixer(x, xc, dt, gz, qkv, y_conf, _pad_lanes(ssd_dt_bias[i]), _pad_lanes(ssd_a_log[i]),
                   jnp.repeat(ssd_d[i], SSD_HEAD_DIM)[None, :], ssd_norm_w[i][None, :], tri, expand, sinks,
                   w_out[i].astype(BF16), final_norm_w[None, :], final_norm=(i == depth - 1))
    return x
```

```python
import functools

import jax
import jax.numpy as jnp
from jax import lax
from jax.experimental import pallas as pl
from jax.experimental.pallas import tpu as pltpu

F32 = jnp.float32
BF16 = jnp.bfloat16

D_MODEL = 1024
SSD_WIDTH = 1024
SSD_HEAD_DIM = 64
SSD_HEADS = 16
SSD_GROUPS = 2
SSD_STATE = 128
SSD_CONV = 4
SSD_CHUNK = 128
SSD_CONV_DIM = SSD_WIDTH + 2 * SSD_GROUPS * SSD_STATE
ATTN_HEAD_DIM = 64
ATTN_WIDTH = 512
ATTN_Q_HEADS = 8
ATTN_KV_HEADS = 2
WINDOW = 128
CONF_WIDTH = 512
CONF_KERNEL = 31
MIX_WIDTH = SSD_WIDTH + ATTN_WIDTH + CONF_WIDTH
EPS = 1e-5

LANES = 128
SUBLANES = 8
VMEM_LIMIT = 56 * 1024 * 1024

LAYER_ROWS = 512
CONF_HALO = 32
SSD_HALO = SUBLANES


def _sigmoid(v):
    return 1.0 / (1.0 + jnp.exp(-v))


def _silu(v):
    return v * _sigmoid(v)


def _dot(a, b):
    return jnp.dot(a, b, preferred_element_type=F32)


def _dot_nt(a, b):
    return lax.dot_general(a, b, (((1,), (1,)), ((), ())), preferred_element_type=F32)


def _rows_ahead(v, o):
    return pltpu.roll(v, v.shape[0] - o, axis=0)


def _in_proj_kernel(x_ref, nw_ref, wz_ref, wxbc_ref, wdt_ref, wqkv_ref, wconf_ref,
                    cw_ref, cb_ref, dww_ref, dwb_ref, lnw_ref, lnb_ref,
                    gz_ref, xc_ref, dt_ref, qkv_ref, yconf_ref,
                    cpad_ref, cshift_ref, xpad_ref):
    tm = x_ref.shape[0]

    @pl.when(pl.program_id(1) == 0)
    def _():
        cpad_ref[0:CONF_HALO, :] = jnp.zeros((CONF_HALO, CONF_WIDTH), F32)
        xpad_ref[0:SSD_HALO, :] = jnp.zeros((SSD_HALO, SSD_CONV_DIM), F32)

    x = x_ref[...]
    h = x * lax.rsqrt(jnp.mean(x * x, axis=-1, keepdims=True) + EPS) * nw_ref[...]
    hb = h.astype(BF16)

    c = _dot(hb, wconf_ref[...])
    cpad_ref[CONF_HALO:CONF_HALO + tm, :] = c[:, :CONF_WIDTH] * _sigmoid(c[:, CONF_WIDTH:])
    padded = cpad_ref[...]
    n_shift = CONF_HALO + tm - SUBLANES
    for r in range(1, SUBLANES):
        cshift_ref[r - 1] = _rows_ahead(padded, r)[0:n_shift]
    acc = jnp.broadcast_to(dwb_ref[...], (tm, CONF_WIDTH))
    base = CONF_HALO - (CONF_KERNEL - 1)
    for k in range(CONF_KERNEL):
        a8, r = divmod(base + k, SUBLANES)
        if r == 0:
            tap = cpad_ref[a8 * SUBLANES:a8 * SUBLANES + tm, :]
        else:
            tap = cshift_ref[r - 1, a8 * SUBLANES:a8 * SUBLANES + tm, :]
        acc = acc + dww_ref[k:k + 1, :] * tap
    cpad_ref[0:CONF_HALO, :] = cpad_ref[tm:tm + CONF_HALO, :]
    mu = jnp.mean(acc, axis=-1, keepdims=True)
    cc = acc - mu
    ln = cc * lax.rsqrt(jnp.mean(cc * cc, axis=-1, keepdims=True) + EPS)
    ln = ln * lnw_ref[...] + lnb_ref[...]

    gz = _silu(_dot(hb, wz_ref[...]))
    gz_ref[...] = gz[:, :SSD_WIDTH + ATTN_WIDTH].astype(BF16)
    yconf_ref[...] = (_silu(ln) * gz[:, SSD_WIDTH + ATTN_WIDTH:]).astype(BF16)

    xbc = _dot(hb, wxbc_ref[...])
    xpad_ref[SSD_HALO:SSD_HALO + tm, :] = xbc
    xpadded = xpad_ref[...]
    acc2 = cb_ref[...] + cw_ref[SSD_CONV - 1:SSD_CONV, :] * xbc
    for k in range(SSD_CONV - 1):
        acc2 = acc2 + cw_ref[k:k + 1, :] * _rows_ahead(xpadded, SSD_HALO - (SSD_CONV - 1) + k)[0:tm]
    xpad_ref[0:SSD_HALO, :] = xpad_ref[tm:tm + SSD_HALO, :]
    xc_ref[...] = _silu(acc2).astype(BF16)

    dt_ref[...] = _dot(hb, wdt_ref[...])
    qkv_ref[...] = _dot(hb, wqkv_ref[...]).astype(BF16)


def _attn_block(q, kprev, vprev, kcur, vcur, gz, sink_ref, kmin):
    w = WINDOW
    lane = lax.broadcasted_iota(jnp.int32, (2 * w, LANES), 1)
    lo = lane < ATTN_HEAD_DIM
    qi = lax.broadcasted_iota(jnp.int32, (2 * w, 2 * w), 0) % w
    kk = lax.broadcasted_iota(jnp.int32, (2 * w, 2 * w), 1)
    rel = qi - (kk - w)
    band = (rel >= 0) & (rel < w) & (kk >= kmin)
    row = lax.broadcasted_iota(jnp.int32, (2 * w, 1), 0)
    zero = jnp.zeros((), BF16)
    lo_f = jnp.where(lo, 1.0, 0.0)
    ones_bd = jnp.concatenate([lo_f, 1.0 - lo_f], axis=0).astype(BF16)
    outs = []
    for g in range(ATTN_KV_HEADS):
        kk2 = jnp.concatenate([kprev[:, g * LANES:(g + 1) * LANES], kcur[:, g * LANES:(g + 1) * LANES]], axis=0)
        vv2 = jnp.concatenate([vprev[:, g * LANES:(g + 1) * LANES], vcur[:, g * LANES:(g + 1) * LANES]], axis=0)
        kbd = jnp.concatenate([jnp.where(lo, kk2, zero), jnp.where(lo, zero, kk2)], axis=0)
        vbd = jnp.concatenate([jnp.where(lo, vv2, zero), jnp.where(lo, zero, vv2)], axis=0)
        qg = q[:, g * 2 * LANES:(g + 1) * 2 * LANES]
        q2 = jnp.concatenate([qg[:, :LANES], qg[:, LANES:]], axis=0)
        s = _dot_nt(q2, kbd) * (ATTN_HEAD_DIM ** -0.5)
        ps, sink_terms = [], []
        for half in range(2):
            sh = jnp.where(band, s[:, half * 2 * w:(half + 1) * 2 * w], -jnp.inf)
            h0 = 4 * g + half
            sk = jnp.where(row < w, sink_ref[h0:h0 + 1, 0:1], sink_ref[h0 + 2:h0 + 3, 0:1])
            m = jnp.maximum(jnp.max(sh, axis=-1, keepdims=True), sk)
            ps.append(jnp.exp(sh - m).astype(BF16))
            sink_terms.append(jnp.exp(sk - m))
        o2 = _dot(jnp.concatenate(ps, axis=1), jnp.concatenate([vbd, ones_bd], axis=1))
        den = o2[:, LANES:] + jnp.where(lo, sink_terms[0], sink_terms[1])
        o2 = o2[:, :LANES] / den
        outs.append(o2[:w])
        outs.append(o2[w:])
    o = jnp.concatenate(outs, axis=1)
    return (o * gz.astype(F32)).astype(BF16)


def _ssd_chunk(xc, dt, gz, dtb, neg_a, dskip, nw, tri, expand, state_ref):
    q = SSD_CHUNK
    gw = SSD_WIDTH // SSD_GROUPS
    xs_b = xc[:, :SSD_WIDTH]
    xs = xs_b.astype(F32)

    dtv = dt + dtb
    dtp = jnp.maximum(dtv, 0.0) + jnp.log1p(jnp.exp(-jnp.abs(dtv)))
    a = dtp * neg_a
    a_hi = a.astype(BF16)
    r1 = a - a_hi.astype(F32)
    a_mid = r1.astype(BF16)
    a_lo = (r1 - a_mid.astype(F32)).astype(BF16)
    acs = _dot(tri, a_hi) + _dot(tri, a_mid) + _dot(tri, a_lo)
    acs_t = acs.T
    dtp_t = dtp.T
    decay_in = jnp.exp(acs)
    w_end = dtp * jnp.exp(acs[q - 1:q, :] - acs)
    w_end_x = _dot(w_end.astype(BF16), expand)
    decay_in_x = _dot(decay_in.astype(BF16), expand)
    xdd = (xs * w_end_x).astype(BF16)

    ii = lax.broadcasted_iota(jnp.int32, (q, q), 0)
    jj = lax.broadcasted_iota(jnp.int32, (q, q), 1)
    causal = ii >= jj
    lane = lax.broadcasted_iota(jnp.int32, (q, LANES), 1)
    lo = lane < SSD_HEAD_DIM
    zero = jnp.zeros((), BF16)
    heads_per_group = SSD_HEADS // SSD_GROUPS
    y_parts = []
    for g in range(SSD_GROUPS):
        bg = xc[:, SSD_WIDTH + g * SSD_STATE:SSD_WIDTH + (g + 1) * SSD_STATE]
        cg = xc[:, SSD_WIDTH + (SSD_GROUPS + g) * SSD_STATE:SSD_WIDTH + (SSD_GROUPS + g + 1) * SSD_STATE]
        cbm = _dot_nt(cg, bg)
        sg = state_ref[g]
        y_off = _dot(cg, sg.astype(BF16))
        upd = lax.dot_general(bg, xdd[:, g * gw:(g + 1) * gw], (((0,), (0,)), ((), ())),
                              preferred_element_type=F32)
        state_ref[g] = decay_in_x[q - 1:q, g * gw:(g + 1) * gw] * sg + upd
        for p in range(heads_per_group // 2):
            ms = []
            for hh in range(2):
                h = g * heads_per_group + 2 * p + hh
                seg = acs[:, h:h + 1] - acs_t[h:h + 1, :]
                lmat = jnp.exp(jnp.where(causal, seg, -jnp.inf))
                ms.append((cbm * lmat * dtp_t[h:h + 1, :]).astype(BF16))
            c0 = g * gw + p * LANES
            xp = xs_b[:, c0:c0 + LANES]
            rhs = jnp.concatenate([jnp.where(lo, xp, zero), jnp.where(lo, zero, xp)], axis=0)
            y_diag = _dot(jnp.concatenate(ms, axis=1), rhs)
            y_parts.append(y_diag + y_off[:, p * LANES:(p + 1) * LANES] * decay_in_x[:, c0:c0 + LANES])
    y = jnp.concatenate(y_parts, axis=1) + dskip * xs

    gated = y * gz.astype(F32)
    outs = []
    for g in range(SSD_GROUPS):
        gg = gated[:, g * gw:(g + 1) * gw]
        outs.append(gg * lax.rsqrt(jnp.mean(gg * gg, axis=-1, keepdims=True) + EPS))
    return (jnp.concatenate(outs, axis=1) * nw).astype(BF16)


def _mixer_kernel(x_ref, xc_ref, dt_ref, gz_ref, qkv_ref, yconf_ref, dtb_ref, alog_ref, dskip_ref, nw_ref,
                  tri_ref, expand_ref, sink_ref, wo_ref, fw_ref, o_ref,
                  state_ref, kprev_ref, vprev_ref, ys_ref, ya_ref, *, final_norm):
    q = SSD_CHUNK
    w = WINDOW
    k0, v0 = ATTN_WIDTH, ATTN_WIDTH + 2 * LANES
    step = pl.program_id(1)

    @pl.when(step == 0)
    def _():
        state_ref[...] = jnp.zeros(state_ref.shape, F32)
        kprev_ref[...] = jnp.zeros(kprev_ref.shape, BF16)
        vprev_ref[...] = jnp.zeros(vprev_ref.shape, BF16)

    neg_a = -jnp.exp(alog_ref[...])
    nblk = xc_ref.shape[0] // q
    for c in range(nblk):
        rows = slice(c * q, (c + 1) * q)
        ys_ref[rows, :] = _ssd_chunk(xc_ref[rows, :], dt_ref[rows, :], gz_ref[rows, 0:SSD_WIDTH], dtb_ref[...],
                                     neg_a, dskip_ref[...], nw_ref[...], tri_ref[...], expand_ref[...], state_ref)
        if c == 0:
            kprev, vprev = kprev_ref, vprev_ref
            kmin = jnp.where(step > 0, 0, w)
        else:
            prev = slice((c - 1) * w, c * w)
            kprev, vprev = qkv_ref[prev, k0:k0 + 2 * LANES], qkv_ref[prev, v0:v0 + 2 * LANES]
            kmin = 0
        ya_ref[rows, :] = _attn_block(qkv_ref[rows, 0:ATTN_WIDTH], kprev, vprev,
                                      qkv_ref[rows, k0:k0 + 2 * LANES], qkv_ref[rows, v0:v0 + 2 * LANES],
                                      gz_ref[rows, SSD_WIDTH:SSD_WIDTH + ATTN_WIDTH], sink_ref, kmin)
    last = slice((nblk - 1) * w, nblk * w)
    kprev_ref[...] = qkv_ref[last, k0:k0 + 2 * LANES]
    vprev_ref[...] = qkv_ref[last, v0:v0 + 2 * LANES]

    a0 = SSD_WIDTH + ATTN_WIDTH
    acc = (_dot(ys_ref[...], wo_ref[0:SSD_WIDTH, :]) + _dot(ya_ref[...], wo_ref[SSD_WIDTH:a0, :])
           + _dot(yconf_ref[...], wo_ref[a0:, :]))
    xn = x_ref[...] + acc
    if final_norm:
        xn = xn * lax.rsqrt(jnp.mean(xn * xn, axis=-1, keepdims=True) + EPS) * fw_ref[...]
    o_ref[...] = xn


N_PROJ_CONSTS = 12
N_MIX_CONSTS = 9


def _layer_kernel(*refs, final_norm):
    x_ref = refs[0]
    proj_consts = refs[1:1 + N_PROJ_CONSTS]
    mix_consts = refs[1 + N_PROJ_CONSTS:1 + N_PROJ_CONSTS + N_MIX_CONSTS]
    o_ref = refs[1 + N_PROJ_CONSTS + N_MIX_CONSTS]
    (gz_ref, xc_ref, dt_ref, qkv_ref, yconf_ref, cpad_ref, cshift_ref, xpad_ref,
     state_ref, kprev_ref, vprev_ref, ys_ref, ya_ref) = refs[2 + N_PROJ_CONSTS + N_MIX_CONSTS:]
    _in_proj_kernel(x_ref, *proj_consts, gz_ref, xc_ref, dt_ref, qkv_ref, yconf_ref,
                    cpad_ref, cshift_ref, xpad_ref)
    _mixer_kernel(x_ref, xc_ref, dt_ref, gz_ref, qkv_ref, yconf_ref, *mix_consts, o_ref,
                  state_ref, kprev_ref, vprev_ref, ys_ref, ya_ref, final_norm=final_norm)


def _layer(x, proj_consts, mix_consts, final_norm):
    b, l, d = x.shape
    tm = min(LAYER_ROWS, l)
    assert len(proj_consts) == N_PROJ_CONSTS and len(mix_consts) == N_MIX_CONSTS
    row = pl.BlockSpec((None, tm, d), lambda i, j: (i, j, 0))
    full = lambda a: pl.BlockSpec(a.shape, lambda i, j: (0, 0), pipeline_mode=pl.Buffered(1))
    qkv_w = proj_consts[4].shape[1]
    return pl.pallas_call(
        functools.partial(_layer_kernel, final_norm=final_norm),
        grid=(b, l // tm),
        in_specs=[row] + [full(a) for a in proj_consts + mix_consts],
        out_specs=row,
        out_shape=jax.ShapeDtypeStruct((b, l, d), F32),
        scratch_shapes=[pltpu.VMEM((tm, SSD_WIDTH + ATTN_WIDTH), BF16),
                        pltpu.VMEM((tm, SSD_CONV_DIM), BF16),
                        pltpu.VMEM((tm, LANES), F32),
                        pltpu.VMEM((tm, qkv_w), BF16),
                        pltpu.VMEM((tm, CONF_WIDTH), BF16),
                        pltpu.VMEM((CONF_HALO + tm, CONF_WIDTH), F32),
                        pltpu.VMEM((SUBLANES - 1, CONF_HALO + tm - SUBLANES, CONF_WIDTH), F32),
                        pltpu.VMEM((SSD_HALO + tm, SSD_CONV_DIM), F32),
                        pltpu.VMEM((SSD_GROUPS, SSD_STATE, SSD_WIDTH // SSD_GROUPS), F32),
                        pltpu.VMEM((WINDOW, 2 * LANES), BF16),
                        pltpu.VMEM((WINDOW, 2 * LANES), BF16),
                        pltpu.VMEM((tm, SSD_WIDTH), BF16),
                        pltpu.VMEM((tm, ATTN_WIDTH), BF16)],
        compiler_params=pltpu.CompilerParams(dimension_semantics=("parallel", "arbitrary"),
                                             vmem_limit_bytes=VMEM_LIMIT),
        name="layer",
    )(x, *proj_consts, *mix_consts)


def _pad_lanes(v, n=LANES):
    return jnp.pad(v.astype(F32), (0, n - v.shape[0]))[None, :]


def kernel(x, norm_w, w_in, ssd_conv_w, ssd_conv_b, ssd_dt_bias, ssd_a_log, ssd_d, ssd_norm_w, attn_sinks, conf_dw_w, conf_dw_b, conf_ln_w, conf_ln_b, w_out, final_norm_w):
    b, l, d = x.shape
    depth = w_in.shape[0]
    kv = ATTN_KV_HEADS * ATTN_HEAD_DIM
    o_xbc = MIX_WIDTH
    o_dt = o_xbc + SSD_CONV_DIM
    o_q = o_dt + SSD_HEADS
    o_k = o_q + ATTN_WIDTH
    o_v = o_k + kv
    o_conf = o_v + kv

    q = SSD_CHUNK
    tri = (lax.broadcasted_iota(jnp.int32, (q, q), 0) >= lax.broadcasted_iota(jnp.int32, (q, q), 1)).astype(BF16)
    head_of_lane = lax.broadcasted_iota(jnp.int32, (LANES, SSD_WIDTH), 1) // SSD_HEAD_DIM
    expand = (lax.broadcasted_iota(jnp.int32, (LANES, SSD_WIDTH), 0) == head_of_lane).astype(BF16)

    def dup_heads(wm):
        wm = wm.reshape(d, ATTN_KV_HEADS, 1, ATTN_HEAD_DIM)
        return jnp.broadcast_to(wm, (d, ATTN_KV_HEADS, 2, ATTN_HEAD_DIM)).reshape(d, 2 * kv)

    w_in_b = w_in.astype(BF16)
    w_out_b = w_out.astype(BF16)
    for i in range(depth):
        wi = w_in_b[i]
        wdt = jnp.pad(wi[:, o_dt:o_q], ((0, 0), (0, LANES - SSD_HEADS)))
        wqkv = jnp.concatenate([wi[:, o_q:o_k], dup_heads(wi[:, o_k:o_v]), dup_heads(wi[:, o_v:o_conf])], axis=1)
        proj_consts = (norm_w[i][None, :], wi[:, :o_xbc], wi[:, o_xbc:o_dt], wdt, wqkv, wi[:, o_conf:],
                       ssd_conv_w[i], ssd_conv_b[i][None, :], conf_dw_w[i], conf_dw_b[i][None, :],
                       conf_ln_w[i][None, :], conf_ln_b[i][None, :])
        sinks = jnp.broadcast_to(attn_sinks[i].astype(F32)[:, None], (ATTN_Q_HEADS, LANES))
        mix_consts = (_pad_lanes(ssd_dt_bias[i]), _pad_lanes(ssd_a_log[i]),
                      jnp.repeat(ssd_d[i], SSD_HEAD_DIM)[None, :], ssd_norm_w[i][None, :], tri, expand, sinks,
                      w_out_b[i], final_norm_w[None, :])
        x = _layer(x, proj_consts, mix_consts, final_norm=(i == depth - 1))
    return x
```

```python
import functools

import jax
import jax.numpy as jnp
from jax import lax
from jax.experimental import pallas as pl
from jax.experimental.pallas import tpu as pltpu

F32 = jnp.float32
BF16 = jnp.bfloat16

D_MODEL = 1024
SSD_WIDTH = 1024
SSD_HEAD_DIM = 64
SSD_HEADS = 16
SSD_GROUPS = 2
SSD_STATE = 128
SSD_CONV = 4
SSD_CHUNK = 128
SSD_CONV_DIM = SSD_WIDTH + 2 * SSD_GROUPS * SSD_STATE
ATTN_HEAD_DIM = 64
ATTN_WIDTH = 512
ATTN_Q_HEADS = 8
ATTN_KV_HEADS = 2
WINDOW = 128
CONF_WIDTH = 512
CONF_KERNEL = 31
MIX_WIDTH = SSD_WIDTH + ATTN_WIDTH + CONF_WIDTH
EPS = 1e-5

LANES = 128
SUBLANES = 8
VMEM_LIMIT = 56 * 1024 * 1024

LAYER_ROWS = 512
CONF_HALO = 32
SSD_HALO = SUBLANES


def _sigmoid(v):
    return 1.0 / (1.0 + jnp.exp(-v))


def _silu(v):
    return v * _sigmoid(v)


def _dot(a, b):
    return jnp.dot(a, b, preferred_element_type=F32)


def _dot_nt(a, b):
    return lax.dot_general(a, b, (((1,), (1,)), ((), ())), preferred_element_type=F32)


def _rows_ahead(v, o):
    return pltpu.roll(v, v.shape[0] - o, axis=0)


def _in_proj_kernel(x_ref, nw_ref, wz_ref, wxbc_ref, wdt_ref, wqkv_ref, wconf_ref,
                    cw_ref, cb_ref, dww_ref, dwb_ref, lnw_ref, lnb_ref,
                    gz_ref, xc_ref, dt_ref, qkv_ref, yconf_ref,
                    cpad_ref, cshift_ref, xpad_ref):
    tm = x_ref.shape[0]

    @pl.when(pl.program_id(1) == 0)
    def _():
        cpad_ref[0:CONF_HALO, :] = jnp.zeros((CONF_HALO, CONF_WIDTH), F32)
        xpad_ref[0:SSD_HALO, :] = jnp.zeros((SSD_HALO, SSD_CONV_DIM), F32)

    x = x_ref[...]
    h = x * lax.rsqrt(jnp.mean(x * x, axis=-1, keepdims=True) + EPS) * nw_ref[...]
    hb = h.astype(BF16)

    c = _dot(hb, wconf_ref[...])
    cpad_ref[CONF_HALO:CONF_HALO + tm, :] = c[:, :CONF_WIDTH] * _sigmoid(c[:, CONF_WIDTH:])
    padded = cpad_ref[...]
    n_shift = CONF_HALO + tm - SUBLANES
    for r in range(1, SUBLANES):
        cshift_ref[r - 1] = _rows_ahead(padded, r)[0:n_shift]
    acc = jnp.broadcast_to(dwb_ref[...], (tm, CONF_WIDTH))
    base = CONF_HALO - (CONF_KERNEL - 1)
    for k in range(CONF_KERNEL):
        a8, r = divmod(base + k, SUBLANES)
        if r == 0:
            tap = cpad_ref[a8 * SUBLANES:a8 * SUBLANES + tm, :]
        else:
            tap = cshift_ref[r - 1, a8 * SUBLANES:a8 * SUBLANES + tm, :]
        acc = acc + dww_ref[k:k + 1, :] * tap
    cpad_ref[0:CONF_HALO, :] = cpad_ref[tm:tm + CONF_HALO, :]
    mu = jnp.mean(acc, axis=-1, keepdims=True)
    cc = acc - mu
    ln = cc * lax.rsqrt(jnp.mean(cc * cc, axis=-1, keepdims=True) + EPS)
    ln = ln * lnw_ref[...] + lnb_ref[...]

    gz = _silu(_dot(hb, wz_ref[...]))
    gz_ref[...] = gz[:, :SSD_WIDTH + ATTN_WIDTH].astype(BF16)
    yconf_ref[...] = (_silu(ln) * gz[:, SSD_WIDTH + ATTN_WIDTH:]).astype(BF16)

    xbc = _dot(hb, wxbc_ref[...])
    xpad_ref[SSD_HALO:SSD_HALO + tm, :] = xbc
    xpadded = xpad_ref[...]
    acc2 = cb_ref[...] + cw_ref[SSD_CONV - 1:SSD_CONV, :] * xbc
    for k in range(SSD_CONV - 1):
        acc2 = acc2 + cw_ref[k:k + 1, :] * _rows_ahead(xpadded, SSD_HALO - (SSD_CONV - 1) + k)[0:tm]
    xpad_ref[0:SSD_HALO, :] = xpad_ref[tm:tm + SSD_HALO, :]
    xc_ref[...] = _silu(acc2).astype(BF16)

    dt_ref[...] = _dot(hb, wdt_ref[...])
    qkv_ref[...] = _dot(hb, wqkv_ref[...]).astype(BF16)


def _attn_block(q, kprev, vprev, kcur, vcur, gz, sink_ref, kmin):
    w = WINDOW
    lane = lax.broadcasted_iota(jnp.int32, (2 * w, LANES), 1)
    lo = lane < ATTN_HEAD_DIM
    qi = lax.broadcasted_iota(jnp.int32, (2 * w, 2 * w), 0) % w
    kk = lax.broadcasted_iota(jnp.int32, (2 * w, 2 * w), 1)
    rel = qi - (kk - w)
    band = (rel >= 0) & (rel < w) & (kk >= kmin)
    row = lax.broadcasted_iota(jnp.int32, (2 * w, 1), 0)
    zero = jnp.zeros((), BF16)
    lo_f = jnp.where(lo, 1.0, 0.0)
    ones_bd = jnp.concatenate([lo_f, 1.0 - lo_f], axis=0).astype(BF16)
    outs = []
    for g in range(ATTN_KV_HEADS):
        kk2 = jnp.concatenate([kprev[:, g * LANES:(g + 1) * LANES], kcur[:, g * LANES:(g + 1) * LANES]], axis=0)
        vv2 = jnp.concatenate([vprev[:, g * LANES:(g + 1) * LANES], vcur[:, g * LANES:(g + 1) * LANES]], axis=0)
        kbd = jnp.concatenate([jnp.where(lo, kk2, zero), jnp.where(lo, zero, kk2)], axis=0)
        vbd = jnp.concatenate([jnp.where(lo, vv2, zero), jnp.where(lo, zero, vv2)], axis=0)
        qg = q[:, g * 2 * LANES:(g + 1) * 2 * LANES]
        q2 = jnp.concatenate([qg[:, :LANES], qg[:, LANES:]], axis=0)
        s = _dot_nt(q2, kbd) * (ATTN_HEAD_DIM ** -0.5)
        ps, sink_terms = [], []
        for half in range(2):
            sh = jnp.where(band, s[:, half * 2 * w:(half + 1) * 2 * w], -jnp.inf)
            h0 = 4 * g + half
            sk = jnp.where(row < w, sink_ref[h0:h0 + 1, 0:1], sink_ref[h0 + 2:h0 + 3, 0:1])
            m = jnp.maximum(jnp.max(sh, axis=-1, keepdims=True), sk)
            ps.append(jnp.exp(sh - m).astype(BF16))
            sink_terms.append(jnp.exp(sk - m))
        o2 = _dot(jnp.concatenate(ps, axis=1), jnp.concatenate([vbd, ones_bd], axis=1))
        den = o2[:, LANES:] + jnp.where(lo, sink_terms[0], sink_terms[1])
        o2 = o2[:, :LANES] / den
        outs.append(o2[:w])
        outs.append(o2[w:])
    o = jnp.concatenate(outs, axis=1)
    return (o * gz.astype(F32)).astype(BF16)


def _ssd_chunk(xc, dt, gz, dtb, neg_a, dskip, nw, tri, expand, state_ref):
    q = SSD_CHUNK
    gw = SSD_WIDTH // SSD_GROUPS
    xs_b = xc[:, :SSD_WIDTH]
    xs = xs_b.astype(F32)

    dtv = dt + dtb
    dtp = jnp.maximum(dtv, 0.0) + jnp.log1p(jnp.exp(-jnp.abs(dtv)))
    a = dtp * neg_a
    a_hi = a.astype(BF16)
    r1 = a - a_hi.astype(F32)
    a_mid = r1.astype(BF16)
    a_lo = (r1 - a_mid.astype(F32)).astype(BF16)
    acs = _dot(tri, a_hi) + _dot(tri, a_mid) + _dot(tri, a_lo)
    acs_t = acs.T
    dtp_t = dtp.T
    decay_in = jnp.exp(acs)
    w_end = dtp * jnp.exp(acs[q - 1:q, :] - acs)
    w_end_x = _dot(w_end.astype(BF16), expand)
    decay_in_x = _dot(decay_in.astype(BF16), expand)
    xdd = (xs * w_end_x).astype(BF16)

    ii = lax.broadcasted_iota(jnp.int32, (q, q), 0)
    jj = lax.broadcasted_iota(jnp.int32, (q, q), 1)
    causal = ii >= jj
    lane = lax.broadcasted_iota(jnp.int32, (q, LANES), 1)
    lo = lane < SSD_HEAD_DIM
    zero = jnp.zeros((), BF16)
    heads_per_group = SSD_HEADS // SSD_GROUPS
    y_parts = []
    for g in range(SSD_GROUPS):
        bg = xc[:, SSD_WIDTH + g * SSD_STATE:SSD_WIDTH + (g + 1) * SSD_STATE]
        cg = xc[:, SSD_WIDTH + (SSD_GROUPS + g) * SSD_STATE:SSD_WIDTH + (SSD_GROUPS + g + 1) * SSD_STATE]
        cbm = _dot_nt(cg, bg)
        sg = state_ref[g]
        y_off = _dot(cg, sg.astype(BF16))
        upd = lax.dot_general(bg, xdd[:, g * gw:(g + 1) * gw], (((0,), (0,)), ((), ())),
                              preferred_element_type=F32)
        state_ref[g] = decay_in_x[q - 1:q, g * gw:(g + 1) * gw] * sg + upd
        for p in range(heads_per_group // 2):
            ms = []
            for hh in range(2):
                h = g * heads_per_group + 2 * p + hh
                seg = acs[:, h:h + 1] - acs_t[h:h + 1, :]
                lmat = jnp.exp(jnp.where(causal, seg, -jnp.inf))
                ms.append((cbm * lmat * dtp_t[h:h + 1, :]).astype(BF16))
            c0 = g * gw + p * LANES
            xp = xs_b[:, c0:c0 + LANES]
            rhs = jnp.concatenate([jnp.where(lo, xp, zero), jnp.where(lo, zero, xp)], axis=0)
            y_diag = _dot(jnp.concatenate(ms, axis=1), rhs)
            y_parts.append(y_diag + y_off[:, p * LANES:(p + 1) * LANES] * decay_in_x[:, c0:c0 + LANES])
    y = jnp.concatenate(y_parts, axis=1) + dskip * xs

    gated = y * gz.astype(F32)
    outs = []
    for g in range(SSD_GROUPS):
        gg = gated[:, g * gw:(g + 1) * gw]
        outs.append(gg * lax.rsqrt(jnp.mean(gg * gg, axis=-1, keepdims=True) + EPS))
    return (jnp.concatenate(outs, axis=1) * nw).astype(BF16)


def _mixer_kernel(x_ref, xc_ref, dt_ref, gz_ref, qkv_ref, yconf_ref, dtb_ref, alog_ref, dskip_ref, nw_ref,
                  tri_ref, expand_ref, sink_ref, wo_ref, fw_ref, o_ref,
                  state_ref, kprev_ref, vprev_ref, ys_ref, ya_ref, *, final_norm):
    q = SSD_CHUNK
    w = WINDOW
    k0, v0 = ATTN_WIDTH, ATTN_WIDTH + 2 * LANES
    step = pl.program_id(1)

    @pl.when(step == 0)
    def _():
        state_ref[...] = jnp.zeros(state_ref.shape, F32)
        kprev_ref[...] = jnp.zeros(kprev_ref.shape, BF16)
        vprev_ref[...] = jnp.zeros(vprev_ref.shape, BF16)

    neg_a = -jnp.exp(alog_ref[...])
    nblk = xc_ref.shape[0] // q
    for c in range(nblk):
        rows = slice(c * q, (c + 1) * q)
        ys_ref[rows, :] = _ssd_chunk(xc_ref[rows, :], dt_ref[rows, :], gz_ref[rows, 0:SSD_WIDTH], dtb_ref[...],
                                     neg_a, dskip_ref[...], nw_ref[...], tri_ref[...], expand_ref[...], state_ref)
        if c == 0:
            kprev, vprev = kprev_ref, vprev_ref
            kmin = jnp.where(step > 0, 0, w)
        else:
            prev = slice((c - 1) * w, c * w)
            kprev, vprev = qkv_ref[prev, k0:k0 + 2 * LANES], qkv_ref[prev, v0:v0 + 2 * LANES]
            kmin = 0
        ya_ref[rows, :] = _attn_block(qkv_ref[rows, 0:ATTN_WIDTH], kprev, vprev,
                                      qkv_ref[rows, k0:k0 + 2 * LANES], qkv_ref[rows, v0:v0 + 2 * LANES],
                                      gz_ref[rows, SSD_WIDTH:SSD_WIDTH + ATTN_WIDTH], sink_ref, kmin)
    last = slice((nblk - 1) * w, nblk * w)
    kprev_ref[...] = qkv_ref[last, k0:k0 + 2 * LANES]
    vprev_ref[...] = qkv_ref[last, v0:v0 + 2 * LANES]

    a0 = SSD_WIDTH + ATTN_WIDTH
    acc = (_dot(ys_ref[...], wo_ref[0:SSD_WIDTH, :]) + _dot(ya_ref[...], wo_ref[SSD_WIDTH:a0, :])
           + _dot(yconf_ref[...], wo_ref[a0:, :]))
    xn = x_ref[...] + acc
    if final_norm:
        xn = xn * lax.rsqrt(jnp.mean(xn * xn, axis=-1, keepdims=True) + EPS) * fw_ref[...]
    o_ref[...] = xn


N_PROJ_CONSTS = 8
N_MIX_CONSTS = 9

QKV_COLS = ATTN_WIDTH + 4 * ATTN_KV_HEADS * ATTN_HEAD_DIM
PACK_Z = 0
PACK_XBC = PACK_Z + MIX_WIDTH
PACK_CONF = PACK_XBC + SSD_CONV_DIM
PACK_QKV = PACK_CONF + 2 * CONF_WIDTH
PACK_DT = PACK_QKV + QKV_COLS
PACK_COLS = PACK_DT + LANES


def _layer_kernel(*refs, final_norm):
    x_ref = refs[0]
    nw_ref, w_ref, cw_ref, cb_ref, dww_ref, dwb_ref, lnw_ref, lnb_ref = refs[1:1 + N_PROJ_CONSTS]
    mix_consts = refs[1 + N_PROJ_CONSTS:1 + N_PROJ_CONSTS + N_MIX_CONSTS]
    o_ref = refs[1 + N_PROJ_CONSTS + N_MIX_CONSTS]
    (gz_ref, xc_ref, dt_ref, qkv_ref, yconf_ref, cpad_ref, cshift_ref, xpad_ref,
     state_ref, kprev_ref, vprev_ref, ys_ref, ya_ref) = refs[2 + N_PROJ_CONSTS + N_MIX_CONSTS:]
    seg = lambda a, b: w_ref.at[:, a:b]
    _in_proj_kernel(x_ref, nw_ref, seg(PACK_Z, PACK_XBC), seg(PACK_XBC, PACK_CONF), seg(PACK_DT, PACK_COLS),
                    seg(PACK_QKV, PACK_DT), seg(PACK_CONF, PACK_QKV),
                    cw_ref, cb_ref, dww_ref, dwb_ref, lnw_ref, lnb_ref,
                    gz_ref, xc_ref, dt_ref, qkv_ref, yconf_ref, cpad_ref, cshift_ref, xpad_ref)
    _mixer_kernel(x_ref, xc_ref, dt_ref, gz_ref, qkv_ref, yconf_ref, *mix_consts, o_ref,
                  state_ref, kprev_ref, vprev_ref, ys_ref, ya_ref, final_norm=final_norm)


def _layer(x, layer, proj_consts, mix_consts, final_norm):
    b, l, d = x.shape
    tm = min(LAYER_ROWS, l)
    assert len(proj_consts) == N_PROJ_CONSTS and len(mix_consts) == N_MIX_CONSTS
    row = pl.BlockSpec((None, tm, d), lambda i, j: (i, j, 0))

    def full(a):
        if a.ndim == 3:
            return pl.BlockSpec((None,) + a.shape[1:], lambda i, j: (layer, 0, 0), pipeline_mode=pl.Buffered(1))
        return pl.BlockSpec(a.shape, lambda i, j: (0, 0), pipeline_mode=pl.Buffered(1))

    qkv_w = QKV_COLS
    return pl.pallas_call(
        functools.partial(_layer_kernel, final_norm=final_norm),
        grid=(b, l // tm),
        in_specs=[row] + [full(a) for a in proj_consts + mix_consts],
        out_specs=row,
        out_shape=jax.ShapeDtypeStruct((b, l, d), F32),
        scratch_shapes=[pltpu.VMEM((tm, SSD_WIDTH + ATTN_WIDTH), BF16),
                        pltpu.VMEM((tm, SSD_CONV_DIM), BF16),
                        pltpu.VMEM((tm, LANES), F32),
                        pltpu.VMEM((tm, qkv_w), BF16),
                        pltpu.VMEM((tm, CONF_WIDTH), BF16),
                        pltpu.VMEM((CONF_HALO + tm, CONF_WIDTH), F32),
                        pltpu.VMEM((SUBLANES - 1, CONF_HALO + tm - SUBLANES, CONF_WIDTH), F32),
                        pltpu.VMEM((SSD_HALO + tm, SSD_CONV_DIM), F32),
                        pltpu.VMEM((SSD_GROUPS, SSD_STATE, SSD_WIDTH // SSD_GROUPS), F32),
                        pltpu.VMEM((WINDOW, 2 * LANES), BF16),
                        pltpu.VMEM((WINDOW, 2 * LANES), BF16),
                        pltpu.VMEM((tm, SSD_WIDTH), BF16),
                        pltpu.VMEM((tm, ATTN_WIDTH), BF16)],
        compiler_params=pltpu.CompilerParams(dimension_semantics=("parallel", "arbitrary"),
                                             vmem_limit_bytes=VMEM_LIMIT),
        name="layer",
    )(x, *proj_consts, *mix_consts)


def _pad_lanes(v, n=LANES):
    return jnp.pad(v.astype(F32), (0, n - v.shape[0]))[None, :]


def kernel(x, norm_w, w_in, ssd_conv_w, ssd_conv_b, ssd_dt_bias, ssd_a_log, ssd_d, ssd_norm_w, attn_sinks, conf_dw_w, conf_dw_b, conf_ln_w, conf_ln_b, w_out, final_norm_w):
    b, l, d = x.shape
    depth = w_in.shape[0]
    kv = ATTN_KV_HEADS * ATTN_HEAD_DIM
    o_xbc = MIX_WIDTH
    o_dt = o_xbc + SSD_CONV_DIM
    o_q = o_dt + SSD_HEADS
    o_k = o_q + ATTN_WIDTH
    o_v = o_k + kv
    o_conf = o_v + kv

    q = SSD_CHUNK
    tri = (lax.broadcasted_iota(jnp.int32, (q, q), 0) >= lax.broadcasted_iota(jnp.int32, (q, q), 1)).astype(BF16)
    head_of_lane = lax.broadcasted_iota(jnp.int32, (LANES, SSD_WIDTH), 1) // SSD_HEAD_DIM
    expand = (lax.broadcasted_iota(jnp.int32, (LANES, SSD_WIDTH), 0) == head_of_lane).astype(BF16)

    def dup_heads(wm):
        wm = wm.reshape(depth, d, ATTN_KV_HEADS, 1, ATTN_HEAD_DIM)
        return jnp.broadcast_to(wm, (depth, d, ATTN_KV_HEADS, 2, ATTN_HEAD_DIM)).reshape(depth, d, 2 * kv)

    dt_pad = jnp.zeros((depth, d, LANES - SSD_HEADS), w_in.dtype)
    w_pack = jnp.concatenate([w_in[:, :, :o_dt], w_in[:, :, o_conf:], w_in[:, :, o_q:o_k],
                              dup_heads(w_in[:, :, o_k:o_v]), dup_heads(w_in[:, :, o_v:o_conf]),
                              w_in[:, :, o_dt:o_q], dt_pad], axis=-1).astype(BF16)
    assert w_pack.shape[-1] == PACK_COLS
    w_out_b = w_out.astype(BF16)
    for i in range(depth):
        proj_consts = (norm_w[i][None, :], w_pack, ssd_conv_w[i], ssd_conv_b[i][None, :],
                       conf_dw_w[i], conf_dw_b[i][None, :], conf_ln_w[i][None, :], conf_ln_b[i][None, :])
        sinks = jnp.broadcast_to(attn_sinks[i].astype(F32)[:, None], (ATTN_Q_HEADS, LANES))
        mix_consts = (_pad_lanes(ssd_dt_bias[i]), _pad_lanes(ssd_a_log[i]),
                      jnp.repeat(ssd_d[i], SSD_HEAD_DIM)[None, :], ssd_norm_w[i][None, :], tri, expand, sinks,
                      w_out_b, final_norm_w[None, :])
        x = _layer(x, i, proj_consts, mix_consts, final_norm=(i == depth - 1))
    return x
```

```python
import functools

import jax
import jax.numpy as jnp
from jax import lax
from jax.experimental import pallas as pl
from jax.experimental.pallas import tpu as pltpu

F32 = jnp.float32
BF16 = jnp.bfloat16

D_MODEL = 1024
SSD_WIDTH = 1024
SSD_HEAD_DIM = 64
SSD_HEADS = 16
SSD_GROUPS = 2
SSD_STATE = 128
SSD_CONV = 4
SSD_CHUNK = 128
SSD_CONV_DIM = SSD_WIDTH + 2 * SSD_GROUPS * SSD_STATE
ATTN_HEAD_DIM = 64
ATTN_WIDTH = 512
ATTN_Q_HEADS = 8
ATTN_KV_HEADS = 2
WINDOW = 128
CONF_WIDTH = 512
CONF_KERNEL = 31
MIX_WIDTH = SSD_WIDTH + ATTN_WIDTH + CONF_WIDTH
EPS = 1e-5

LANES = 128
SUBLANES = 8
VMEM_LIMIT = 56 * 1024 * 1024

LAYER_ROWS = 512
CONF_HALO = 32
SSD_HALO = SUBLANES


def _sigmoid(v):
    return 1.0 / (1.0 + jnp.exp(-v))


def _silu(v):
    return v * _sigmoid(v)


def _dot(a, b):
    return jnp.dot(a, b, preferred_element_type=F32)


def _dot_nt(a, b):
    return lax.dot_general(a, b, (((1,), (1,)), ((), ())), preferred_element_type=F32)


def _rows_ahead(v, o):
    return pltpu.roll(v, v.shape[0] - o, axis=0)


def _in_proj_kernel(x_ref, nw_ref, wz_ref, wxbc_ref, wdt_ref, wqkv_ref, wconf_ref,
                    cw_ref, cb_ref, dww_ref, dwb_ref, lnw_ref, lnb_ref,
                    gz_ref, xc_ref, dt_ref, qkv_ref, yconf_ref,
                    cpad_ref, cshift_ref, xpad_ref):
    tm = x_ref.shape[0]

    @pl.when(pl.program_id(1) == 0)
    def _():
        cpad_ref[0:CONF_HALO, :] = jnp.zeros((CONF_HALO, CONF_WIDTH), F32)
        xpad_ref[0:SSD_HALO, :] = jnp.zeros((SSD_HALO, SSD_CONV_DIM), F32)

    x = x_ref[...]
    h = x * lax.rsqrt(jnp.mean(x * x, axis=-1, keepdims=True) + EPS) * nw_ref[...]
    hb = h.astype(BF16)

    c = _dot(hb, wconf_ref[...])
    cpad_ref[CONF_HALO:CONF_HALO + tm, :] = c[:, :CONF_WIDTH] * _sigmoid(c[:, CONF_WIDTH:])
    padded = cpad_ref[...]
    n_shift = CONF_HALO + tm - SUBLANES
    for r in range(1, SUBLANES):
        cshift_ref[r - 1] = _rows_ahead(padded, r)[0:n_shift]
    acc = jnp.broadcast_to(dwb_ref[...], (tm, CONF_WIDTH))
    base = CONF_HALO - (CONF_KERNEL - 1)
    for k in range(CONF_KERNEL):
        a8, r = divmod(base + k, SUBLANES)
        if r == 0:
            tap = cpad_ref[a8 * SUBLANES:a8 * SUBLANES + tm, :]
        else:
            tap = cshift_ref[r - 1, a8 * SUBLANES:a8 * SUBLANES + tm, :]
        acc = acc + dww_ref[k:k + 1, :] * tap
    cpad_ref[0:CONF_HALO, :] = cpad_ref[tm:tm + CONF_HALO, :]
    mu = jnp.mean(acc, axis=-1, keepdims=True)
    cc = acc - mu
    ln = cc * lax.rsqrt(jnp.mean(cc * cc, axis=-1, keepdims=True) + EPS)
    ln = ln * lnw_ref[...] + lnb_ref[...]

    gz = _silu(_dot(hb, wz_ref[...]))
    gz_ref[...] = gz[:, :SSD_WIDTH + ATTN_WIDTH].astype(BF16)
    yconf_ref[...] = (_silu(ln) * gz[:, SSD_WIDTH + ATTN_WIDTH:]).astype(BF16)

    xbc = _dot(hb, wxbc_ref[...])
    xpad_ref[SSD_HALO:SSD_HALO + tm, :] = xbc
    xpadded = xpad_ref[...]
    acc2 = cb_ref[...] + cw_ref[SSD_CONV - 1:SSD_CONV, :] * xbc
    for k in range(SSD_CONV - 1):
        acc2 = acc2 + cw_ref[k:k + 1, :] * _rows_ahead(xpadded, SSD_HALO - (SSD_CONV - 1) + k)[0:tm]
    xpad_ref[0:SSD_HALO, :] = xpad_ref[tm:tm + SSD_HALO, :]
    xc_ref[...] = _silu(acc2).astype(BF16)

    dt_ref[...] = _dot(hb, wdt_ref[...])
    qkv_ref[...] = _dot(hb, wqkv_ref[...]).astype(BF16)


def _attn_block(q, kprev, vprev, kcur, vcur, gz, sink_ref, kmin):
    w = WINDOW
    lane = lax.broadcasted_iota(jnp.int32, (2 * w, LANES), 1)
    lo = lane < ATTN_HEAD_DIM
    qi = lax.broadcasted_iota(jnp.int32, (2 * w, 2 * w), 0) % w
    kk = lax.broadcasted_iota(jnp.int32, (2 * w, 2 * w), 1)
    rel = qi - (kk - w)
    band = (rel >= 0) & (rel < w) & (kk >= kmin)
    row = lax.broadcasted_iota(jnp.int32, (2 * w, 1), 0)
    zero = jnp.zeros((), BF16)
    lo_f = jnp.where(lo, 1.0, 0.0)
    ones_bd = jnp.concatenate([lo_f, 1.0 - lo_f], axis=0).astype(BF16)
    outs = []
    for g in range(ATTN_KV_HEADS):
        kk2 = jnp.concatenate([kprev[:, g * LANES:(g + 1) * LANES], kcur[:, g * LANES:(g + 1) * LANES]], axis=0)
        vv2 = jnp.concatenate([vprev[:, g * LANES:(g + 1) * LANES], vcur[:, g * LANES:(g + 1) * LANES]], axis=0)
        kbd = jnp.concatenate([jnp.where(lo, kk2, zero), jnp.where(lo, zero, kk2)], axis=0)
        vbd = jnp.concatenate([jnp.where(lo, vv2, zero), jnp.where(lo, zero, vv2)], axis=0)
        qg = q[:, g * 2 * LANES:(g + 1) * 2 * LANES]
        q2 = jnp.concatenate([qg[:, :LANES], qg[:, LANES:]], axis=0)
        s = _dot_nt(q2, kbd) * (ATTN_HEAD_DIM ** -0.5)
        ps, sink_terms = [], []
        for half in range(2):
            sh = jnp.where(band, s[:, half * 2 * w:(half + 1) * 2 * w], -jnp.inf)
            h0 = 4 * g + half
            sk = jnp.where(row < w, sink_ref[h0:h0 + 1, 0:1], sink_ref[h0 + 2:h0 + 3, 0:1])
            m = jnp.maximum(jnp.max(sh, axis=-1, keepdims=True), sk)
            ps.append(jnp.exp(sh - m).astype(BF16))
            sink_terms.append(jnp.exp(sk - m))
        o2 = _dot(jnp.concatenate(ps, axis=1), jnp.concatenate([vbd, ones_bd], axis=1))
        den = o2[:, LANES:] + jnp.where(lo, sink_terms[0], sink_terms[1])
        o2 = o2[:, :LANES] / den
        outs.append(o2[:w])
        outs.append(o2[w:])
    o = jnp.concatenate(outs, axis=1)
    return (o * gz.astype(F32)).astype(BF16)


def _ssd_chunk(xc, dt, gz, dtb, neg_a, dskip, nw, tri, expand, state_ref):
    q = SSD_CHUNK
    gw = SSD_WIDTH // SSD_GROUPS
    xs_b = xc[:, :SSD_WIDTH]
    xs = xs_b.astype(F32)

    dtv = dt + dtb
    dtp = jnp.maximum(dtv, 0.0) + jnp.log1p(jnp.exp(-jnp.abs(dtv)))
    a = dtp * neg_a
    a_hi = a.astype(BF16)
    r1 = a - a_hi.astype(F32)
    a_mid = r1.astype(BF16)
    a_lo = (r1 - a_mid.astype(F32)).astype(BF16)
    acs = _dot(tri, a_hi) + _dot(tri, a_mid) + _dot(tri, a_lo)
    acs_t = acs.T
    dtp_t = dtp.T
    decay_in = jnp.exp(acs)
    w_end = dtp * jnp.exp(acs[q - 1:q, :] - acs)
    w_end_x = _dot(w_end.astype(BF16), expand)
    decay_in_x = _dot(decay_in.astype(BF16), expand)
    xdd = (xs * w_end_x).astype(BF16)

    ii = lax.broadcasted_iota(jnp.int32, (q, q), 0)
    jj = lax.broadcasted_iota(jnp.int32, (q, q), 1)
    causal = ii >= jj
    lane = lax.broadcasted_iota(jnp.int32, (q, LANES), 1)
    lo = lane < SSD_HEAD_DIM
    zero = jnp.zeros((), BF16)
    heads_per_group = SSD_HEADS // SSD_GROUPS
    y_parts = []
    for g in range(SSD_GROUPS):
        bg = xc[:, SSD_WIDTH + g * SSD_STATE:SSD_WIDTH + (g + 1) * SSD_STATE]
        cg = xc[:, SSD_WIDTH + (SSD_GROUPS + g) * SSD_STATE:SSD_WIDTH + (SSD_GROUPS + g + 1) * SSD_STATE]
        cbm = _dot_nt(cg, bg)
        sg = state_ref[g]
        y_off = _dot(cg, sg.astype(BF16))
        upd = lax.dot_general(bg, xdd[:, g * gw:(g + 1) * gw], (((0,), (0,)), ((), ())),
                              preferred_element_type=F32)
        state_ref[g] = decay_in_x[q - 1:q, g * gw:(g + 1) * gw] * sg + upd
        for p in range(heads_per_group // 2):
            ms = []
            for hh in range(2):
                h = g * heads_per_group + 2 * p + hh
                seg = acs[:, h:h + 1] - acs_t[h:h + 1, :]
                lmat = jnp.exp(jnp.where(causal, seg, -jnp.inf))
                ms.append((cbm * lmat * dtp_t[h:h + 1, :]).astype(BF16))
            c0 = g * gw + p * LANES
            xp = xs_b[:, c0:c0 + LANES]
            rhs = jnp.concatenate([jnp.where(lo, xp, zero), jnp.where(lo, zero, xp)], axis=0)
            y_diag = _dot(jnp.concatenate(ms, axis=1), rhs)
            y_parts.append(y_diag + y_off[:, p * LANES:(p + 1) * LANES] * decay_in_x[:, c0:c0 + LANES])
    y = jnp.concatenate(y_parts, axis=1) + dskip * xs

    gated = y * gz.astype(F32)
    outs = []
    for g in range(SSD_GROUPS):
        gg = gated[:, g * gw:(g + 1) * gw]
        outs.append(gg * lax.rsqrt(jnp.mean(gg * gg, axis=-1, keepdims=True) + EPS))
    return (jnp.concatenate(outs, axis=1) * nw).astype(BF16)


def _mixer_kernel(x_ref, xc_ref, dt_ref, gz_ref, qkv_ref, yconf_ref, dtb_ref, alog_ref, dskip_ref, nw_ref,
                  tri_ref, expand_ref, sink_ref, wo_ref, fw_ref, o_ref,
                  state_ref, kprev_ref, vprev_ref, ys_ref, ya_ref, *, final_norm):
    q = SSD_CHUNK
    w = WINDOW
    k0, v0 = ATTN_WIDTH, ATTN_WIDTH + 2 * LANES
    step = pl.program_id(1)

    @pl.when(step == 0)
    def _():
        state_ref[...] = jnp.zeros(state_ref.shape, F32)
        kprev_ref[...] = jnp.zeros(kprev_ref.shape, BF16)
        vprev_ref[...] = jnp.zeros(vprev_ref.shape, BF16)

    neg_a = -jnp.exp(alog_ref[...])
    nblk = xc_ref.shape[0] // q
    for c in range(nblk):
        rows = slice(c * q, (c + 1) * q)
        ys_ref[rows, :] = _ssd_chunk(xc_ref[rows, :], dt_ref[rows, :], gz_ref[rows, 0:SSD_WIDTH], dtb_ref[...],
                                     neg_a, dskip_ref[...], nw_ref[...], tri_ref[...], expand_ref[...], state_ref)
        if c == 0:
            kprev, vprev = kprev_ref, vprev_ref
            kmin = jnp.where(step > 0, 0, w)
        else:
            prev = slice((c - 1) * w, c * w)
            kprev, vprev = qkv_ref[prev, k0:k0 + 2 * LANES], qkv_ref[prev, v0:v0 + 2 * LANES]
            kmin = 0
        ya_ref[rows, :] = _attn_block(qkv_ref[rows, 0:ATTN_WIDTH], kprev, vprev,
                                      qkv_ref[rows, k0:k0 + 2 * LANES], qkv_ref[rows, v0:v0 + 2 * LANES],
                                      gz_ref[rows, SSD_WIDTH:SSD_WIDTH + ATTN_WIDTH], sink_ref, kmin)
    last = slice((nblk - 1) * w, nblk * w)
    kprev_ref[...] = qkv_ref[last, k0:k0 + 2 * LANES]
    vprev_ref[...] = qkv_ref[last, v0:v0 + 2 * LANES]

    a0 = SSD_WIDTH + ATTN_WIDTH
    acc = (_dot(ys_ref[...], wo_ref[0:SSD_WIDTH, :]) + _dot(ya_ref[...], wo_ref[SSD_WIDTH:a0, :])
           + _dot(yconf_ref[...], wo_ref[a0:, :]))
    xn = x_ref[...] + acc
    if final_norm:
        xn = xn * lax.rsqrt(jnp.mean(xn * xn, axis=-1, keepdims=True) + EPS) * fw_ref[...]
    o_ref[...] = xn


N_PROJ_CONSTS = 9
N_MIX_CONSTS = 9

HEAD_COLS = MIX_WIDTH + SSD_CONV_DIM
QKV_COLS = ATTN_WIDTH + 4 * ATTN_KV_HEADS * ATTN_HEAD_DIM
TAIL_CONF = 0
TAIL_QKV = TAIL_CONF + 2 * CONF_WIDTH
TAIL_DT = TAIL_QKV + QKV_COLS
TAIL_COLS = TAIL_DT + LANES


def _layer_kernel(*refs, final_norm):
    x_ref = refs[0]
    nw_ref, wh_ref, wt_ref, cw_ref, cb_ref, dww_ref, dwb_ref, lnw_ref, lnb_ref = refs[1:1 + N_PROJ_CONSTS]
    mix_consts = refs[1 + N_PROJ_CONSTS:1 + N_PROJ_CONSTS + N_MIX_CONSTS]
    o_ref = refs[1 + N_PROJ_CONSTS + N_MIX_CONSTS]
    (gz_ref, xc_ref, dt_ref, qkv_ref, yconf_ref, cpad_ref, cshift_ref, xpad_ref,
     state_ref, kprev_ref, vprev_ref, ys_ref, ya_ref) = refs[2 + N_PROJ_CONSTS + N_MIX_CONSTS:]
    _in_proj_kernel(x_ref, nw_ref, wh_ref.at[:, 0:MIX_WIDTH], wh_ref.at[:, MIX_WIDTH:HEAD_COLS],
                    wt_ref.at[:, TAIL_DT:TAIL_COLS], wt_ref.at[:, TAIL_QKV:TAIL_DT],
                    wt_ref.at[:, TAIL_CONF:TAIL_QKV],
                    cw_ref, cb_ref, dww_ref, dwb_ref, lnw_ref, lnb_ref,
                    gz_ref, xc_ref, dt_ref, qkv_ref, yconf_ref, cpad_ref, cshift_ref, xpad_ref)
    _mixer_kernel(x_ref, xc_ref, dt_ref, gz_ref, qkv_ref, yconf_ref, *mix_consts, o_ref,
                  state_ref, kprev_ref, vprev_ref, ys_ref, ya_ref, final_norm=final_norm)


def _layer(x, layer, proj_consts, mix_consts, final_norm):
    b, l, d = x.shape
    tm = min(LAYER_ROWS, l)
    assert len(proj_consts) == N_PROJ_CONSTS and len(mix_consts) == N_MIX_CONSTS
    row = pl.BlockSpec((None, tm, d), lambda i, j: (i, j, 0))

    def full(a):
        if a.ndim == 3:
            cols = HEAD_COLS if a is proj_consts[1] else a.shape[2]
            return pl.BlockSpec((None, a.shape[1], cols), lambda i, j: (layer, 0, 0),
                                pipeline_mode=pl.Buffered(1))
        return pl.BlockSpec(a.shape, lambda i, j: (0, 0), pipeline_mode=pl.Buffered(1))

    qkv_w = QKV_COLS
    return pl.pallas_call(
        functools.partial(_layer_kernel, final_norm=final_norm),
        grid=(b, l // tm),
        in_specs=[row] + [full(a) for a in proj_consts + mix_consts],
        out_specs=row,
        out_shape=jax.ShapeDtypeStruct((b, l, d), F32),
        scratch_shapes=[pltpu.VMEM((tm, SSD_WIDTH + ATTN_WIDTH), BF16),
                        pltpu.VMEM((tm, SSD_CONV_DIM), BF16),
                        pltpu.VMEM((tm, LANES), F32),
                        pltpu.VMEM((tm, qkv_w), BF16),
                        pltpu.VMEM((tm, CONF_WIDTH), BF16),
                        pltpu.VMEM((CONF_HALO + tm, CONF_WIDTH), F32),
                        pltpu.VMEM((SUBLANES - 1, CONF_HALO + tm - SUBLANES, CONF_WIDTH), F32),
                        pltpu.VMEM((SSD_HALO + tm, SSD_CONV_DIM), F32),
                        pltpu.VMEM((SSD_GROUPS, SSD_STATE, SSD_WIDTH // SSD_GROUPS), F32),
                        pltpu.VMEM((WINDOW, 2 * LANES), BF16),
                        pltpu.VMEM((WINDOW, 2 * LANES), BF16),
                        pltpu.VMEM((tm, SSD_WIDTH), BF16),
                        pltpu.VMEM((tm, ATTN_WIDTH), BF16)],
        compiler_params=pltpu.CompilerParams(dimension_semantics=("parallel", "arbitrary"),
                                             vmem_limit_bytes=VMEM_LIMIT),
        name="layer",
    )(x, *proj_consts, *mix_consts)


def _pad_lanes(v, n=LANES):
    return jnp.pad(v.astype(F32), (0, n - v.shape[0]))[None, :]


def kernel(x, norm_w, w_in, ssd_conv_w, ssd_conv_b, ssd_dt_bias, ssd_a_log, ssd_d, ssd_norm_w, attn_sinks, conf_dw_w, conf_dw_b, conf_ln_w, conf_ln_b, w_out, final_norm_w):
    b, l, d = x.shape
    depth = w_in.shape[0]
    kv = ATTN_KV_HEADS * ATTN_HEAD_DIM
    o_xbc = MIX_WIDTH
    o_dt = o_xbc + SSD_CONV_DIM
    o_q = o_dt + SSD_HEADS
    o_k = o_q + ATTN_WIDTH
    o_v = o_k + kv
    o_conf = o_v + kv

    q = SSD_CHUNK
    tri = (lax.broadcasted_iota(jnp.int32, (q, q), 0) >= lax.broadcasted_iota(jnp.int32, (q, q), 1)).astype(BF16)
    head_of_lane = lax.broadcasted_iota(jnp.int32, (LANES, SSD_WIDTH), 1) // SSD_HEAD_DIM
    expand = (lax.broadcasted_iota(jnp.int32, (LANES, SSD_WIDTH), 0) == head_of_lane).astype(BF16)

    def dup_heads(wm):
        wm = wm.reshape(depth, d, ATTN_KV_HEADS, 1, ATTN_HEAD_DIM)
        return jnp.broadcast_to(wm, (depth, d, ATTN_KV_HEADS, 2, ATTN_HEAD_DIM)).reshape(depth, d, 2 * kv)

    w_in_b = w_in.astype(BF16)
    dt_pad = jnp.zeros((depth, d, LANES - SSD_HEADS), BF16)
    w_tail = jnp.concatenate([w_in_b[:, :, o_conf:], w_in_b[:, :, o_q:o_k],
                              dup_heads(w_in_b[:, :, o_k:o_v]), dup_heads(w_in_b[:, :, o_v:o_conf]),
                              w_in_b[:, :, o_dt:o_q], dt_pad], axis=-1)
    assert o_dt == HEAD_COLS and w_tail.shape[-1] == TAIL_COLS
    w_out_b = w_out.astype(BF16)
    for i in range(depth):
        proj_consts = (norm_w[i][None, :], w_in_b, w_tail, ssd_conv_w[i], ssd_conv_b[i][None, :],
                       conf_dw_w[i], conf_dw_b[i][None, :], conf_ln_w[i][None, :], conf_ln_b[i][None, :])
        sinks = jnp.broadcast_to(attn_sinks[i].astype(F32)[:, None], (ATTN_Q_HEADS, LANES))
        mix_consts = (_pad_lanes(ssd_dt_bias[i]), _pad_lanes(ssd_a_log[i]),
                      jnp.repeat(ssd_d[i], SSD_HEAD_DIM)[None, :], ssd_norm_w[i][None, :], tri, expand, sinks,
                      w_out_b, final_norm_w[None, :])
        x = _layer(x, i, proj_consts, mix_consts, final_norm=(i == depth - 1))
    return x
```

```python
import functools

import jax
import jax.numpy as jnp
from jax import lax
from jax.experimental import pallas as pl
from jax.experimental.pallas import tpu as pltpu

F32 = jnp.float32
BF16 = jnp.bfloat16

D_MODEL = 1024
SSD_WIDTH = 1024
SSD_HEAD_DIM = 64
SSD_HEADS = 16
SSD_GROUPS = 2
SSD_STATE = 128
SSD_CONV = 4
SSD_CHUNK = 128
SSD_CONV_DIM = SSD_WIDTH + 2 * SSD_GROUPS * SSD_STATE
ATTN_HEAD_DIM = 64
ATTN_WIDTH = 512
ATTN_Q_HEADS = 8
ATTN_KV_HEADS = 2
WINDOW = 128
CONF_WIDTH = 512
CONF_KERNEL = 31
MIX_WIDTH = SSD_WIDTH + ATTN_WIDTH + CONF_WIDTH
EPS = 1e-5
LOG2E = 1.4426950408889634

LANES = 128
SUBLANES = 8
VMEM_LIMIT = 56 * 1024 * 1024

LAYER_ROWS = 512
CONF_HALO = 32
SSD_HALO = SUBLANES


def _sigmoid(v):
    return 1.0 / (1.0 + jnp.exp(-v))


def _silu(v):
    return v * _sigmoid(v)


def _dot(a, b):
    return jnp.dot(a, b, preferred_element_type=F32)


def _dot_nt(a, b):
    return lax.dot_general(a, b, (((1,), (1,)), ((), ())), preferred_element_type=F32)


def _rows_ahead(v, o):
    return pltpu.roll(v, v.shape[0] - o, axis=0)


def _in_proj_kernel(x_ref, nw_ref, wz_ref, wxbc_ref, wdt_ref, wqkv_ref, wconf_ref,
                    cw_ref, cb_ref, dww_ref, dwb_ref, lnw_ref, lnb_ref,
                    gz_ref, xc_ref, dt_ref, qkv_ref, yconf_ref,
                    cpad_ref, cshift_ref, xpad_ref):
    tm = x_ref.shape[0]

    @pl.when(pl.program_id(1) == 0)
    def _():
        cpad_ref[0:CONF_HALO, :] = jnp.zeros((CONF_HALO, CONF_WIDTH), F32)
        xpad_ref[0:SSD_HALO, :] = jnp.zeros((SSD_HALO, SSD_CONV_DIM), F32)

    x = x_ref[...]
    h = x * lax.rsqrt(jnp.mean(x * x, axis=-1, keepdims=True) + EPS) * nw_ref[...]
    hb = h.astype(BF16)

    c = _dot(hb, wconf_ref[...])
    cpad_ref[CONF_HALO:CONF_HALO + tm, :] = c[:, :CONF_WIDTH] * _sigmoid(c[:, CONF_WIDTH:])
    padded = cpad_ref[...]
    n_shift = CONF_HALO + tm - SUBLANES
    for r in range(1, SUBLANES):
        cshift_ref[r - 1] = _rows_ahead(padded, r)[0:n_shift]
    acc = jnp.broadcast_to(dwb_ref[...], (tm, CONF_WIDTH))
    base = CONF_HALO - (CONF_KERNEL - 1)
    for k in range(CONF_KERNEL):
        a8, r = divmod(base + k, SUBLANES)
        if r == 0:
            tap = cpad_ref[a8 * SUBLANES:a8 * SUBLANES + tm, :]
        else:
            tap = cshift_ref[r - 1, a8 * SUBLANES:a8 * SUBLANES + tm, :]
        acc = acc + dww_ref[k:k + 1, :] * tap
    cpad_ref[0:CONF_HALO, :] = cpad_ref[tm:tm + CONF_HALO, :]
    mu = jnp.mean(acc, axis=-1, keepdims=True)
    cc = acc - mu
    ln = cc * lax.rsqrt(jnp.mean(cc * cc, axis=-1, keepdims=True) + EPS)
    ln = ln * lnw_ref[...] + lnb_ref[...]

    gz = _silu(_dot(hb, wz_ref[...]))
    gz_ref[...] = gz[:, :SSD_WIDTH + ATTN_WIDTH].astype(BF16)
    yconf_ref[...] = (_silu(ln) * gz[:, SSD_WIDTH + ATTN_WIDTH:]).astype(BF16)

    xbc = _dot(hb, wxbc_ref[...])
    xpad_ref[SSD_HALO:SSD_HALO + tm, :] = xbc
    xpadded = xpad_ref[...]
    acc2 = cb_ref[...] + cw_ref[SSD_CONV - 1:SSD_CONV, :] * xbc
    for k in range(SSD_CONV - 1):
        acc2 = acc2 + cw_ref[k:k + 1, :] * _rows_ahead(xpadded, SSD_HALO - (SSD_CONV - 1) + k)[0:tm]
    xpad_ref[0:SSD_HALO, :] = xpad_ref[tm:tm + SSD_HALO, :]
    xc_ref[...] = _silu(acc2).astype(BF16)

    dt_ref[...] = _dot(hb, wdt_ref[...])
    qkv_ref[...] = _dot(hb, wqkv_ref[...]).astype(BF16)


def _attn_block(q, kprev, vprev, kcur, vcur, gz, sink_ref, kmin):
    w = WINDOW
    lane = lax.broadcasted_iota(jnp.int32, (2 * w, LANES), 1)
    lo = lane < ATTN_HEAD_DIM
    qi = lax.broadcasted_iota(jnp.int32, (2 * w, 2 * w), 0) % w
    kk = lax.broadcasted_iota(jnp.int32, (2 * w, 2 * w), 1)
    rel = qi - (kk - w)
    band = (rel >= 0) & (rel < w) & (kk >= kmin)
    row = lax.broadcasted_iota(jnp.int32, (2 * w, 1), 0)
    zero = jnp.zeros((), BF16)
    lo_f = jnp.where(lo, 1.0, 0.0)
    ones_bd = jnp.concatenate([lo_f, 1.0 - lo_f], axis=0).astype(BF16)
    outs = []
    for g in range(ATTN_KV_HEADS):
        kk2 = jnp.concatenate([kprev[:, g * LANES:(g + 1) * LANES], kcur[:, g * LANES:(g + 1) * LANES]], axis=0)
        vv2 = jnp.concatenate([vprev[:, g * LANES:(g + 1) * LANES], vcur[:, g * LANES:(g + 1) * LANES]], axis=0)
        kbd = jnp.concatenate([jnp.where(lo, kk2, zero), jnp.where(lo, zero, kk2)], axis=0)
        vbd = jnp.concatenate([jnp.where(lo, vv2, zero), jnp.where(lo, zero, vv2)], axis=0)
        qg = q[:, g * 2 * LANES:(g + 1) * 2 * LANES]
        q2 = jnp.concatenate([qg[:, :LANES], qg[:, LANES:]], axis=0)
        s = _dot_nt(q2, kbd) * (ATTN_HEAD_DIM ** -0.5)
        ps, sink_terms = [], []
        for half in range(2):
            sh = jnp.where(band, s[:, half * 2 * w:(half + 1) * 2 * w], -jnp.inf)
            h0 = 4 * g + half
            sk = jnp.where(row < w, sink_ref[h0:h0 + 1, 0:1], sink_ref[h0 + 2:h0 + 3, 0:1])
            m = jnp.maximum(jnp.max(sh, axis=-1, keepdims=True), sk)
            ps.append(jnp.exp(sh - m).astype(BF16))
            sink_terms.append(jnp.exp(sk - m))
        o2 = _dot(jnp.concatenate(ps, axis=1), jnp.concatenate([vbd, ones_bd], axis=1))
        den = o2[:, LANES:] + jnp.where(lo, sink_terms[0], sink_terms[1])
        o2 = o2[:, :LANES] / den
        outs.append(o2[:w])
        outs.append(o2[w:])
    o = jnp.concatenate(outs, axis=1)
    return (o * gz.astype(F32)).astype(BF16)


def _ssd_chunk(xc, dt, gz, dtb, neg_a, dskip, nw, tri, expand, state_ref):
    q = SSD_CHUNK
    gw = SSD_WIDTH // SSD_GROUPS
    xs_b = xc[:, :SSD_WIDTH]
    xs = xs_b.astype(F32)

    dtv = dt + dtb
    dtp = jnp.maximum(dtv, 0.0) + jnp.log1p(jnp.exp(-jnp.abs(dtv)))
    a = dtp * neg_a
    a_hi = a.astype(BF16)
    r1 = a - a_hi.astype(F32)
    a_mid = r1.astype(BF16)
    a_lo = (r1 - a_mid.astype(F32)).astype(BF16)
    acs = _dot(tri, a_hi) + _dot(tri, a_mid) + _dot(tri, a_lo)
    acs2 = acs * LOG2E
    acs2_t = acs2.T
    dtp_t = dtp.T
    decay_in = jnp.exp(acs)
    w_end = dtp * jnp.exp(acs[q - 1:q, :] - acs)
    w_end_x = _dot(w_end.astype(BF16), expand)
    decay_in_x = _dot(decay_in.astype(BF16), expand)
    xdd = (xs * w_end_x).astype(BF16)

    ii = lax.broadcasted_iota(jnp.int32, (q, q), 0)
    jj = lax.broadcasted_iota(jnp.int32, (q, q), 1)
    causal = ii >= jj
    lane = lax.broadcasted_iota(jnp.int32, (q, LANES), 1)
    lo = lane < SSD_HEAD_DIM
    zero = jnp.zeros((), BF16)
    heads_per_group = SSD_HEADS // SSD_GROUPS
    y_parts = []
    for g in range(SSD_GROUPS):
        bg = xc[:, SSD_WIDTH + g * SSD_STATE:SSD_WIDTH + (g + 1) * SSD_STATE]
        cg = xc[:, SSD_WIDTH + (SSD_GROUPS + g) * SSD_STATE:SSD_WIDTH + (SSD_GROUPS + g + 1) * SSD_STATE]
        cbm = _dot_nt(cg, bg)
        sg = state_ref[g]
        y_off = _dot(cg, sg.astype(BF16))
        upd = lax.dot_general(bg, xdd[:, g * gw:(g + 1) * gw], (((0,), (0,)), ((), ())),
                              preferred_element_type=F32)
        state_ref[g] = decay_in_x[q - 1:q, g * gw:(g + 1) * gw] * sg + upd
        for p in range(heads_per_group // 2):
            ms = []
            for hh in range(2):
                h = g * heads_per_group + 2 * p + hh
                seg = acs2[:, h:h + 1] - acs2_t[h:h + 1, :]
                lmat = jnp.exp2(jnp.where(causal, seg, -jnp.inf))
                ms.append((cbm * lmat * dtp_t[h:h + 1, :]).astype(BF16))
            c0 = g * gw + p * LANES
            xp = xs_b[:, c0:c0 + LANES]
            rhs = jnp.concatenate([jnp.where(lo, xp, zero), jnp.where(lo, zero, xp)], axis=0)
            y_diag = _dot(jnp.concatenate(ms, axis=1), rhs)
            y_parts.append(y_diag + y_off[:, p * LANES:(p + 1) * LANES] * decay_in_x[:, c0:c0 + LANES])
    y = jnp.concatenate(y_parts, axis=1) + dskip * xs

    gated = y * gz.astype(F32)
    outs = []
    for g in range(SSD_GROUPS):
        gg = gated[:, g * gw:(g + 1) * gw]
        outs.append(gg * lax.rsqrt(jnp.mean(gg * gg, axis=-1, keepdims=True) + EPS))
    return (jnp.concatenate(outs, axis=1) * nw).astype(BF16)


def _mixer_kernel(x_ref, xc_ref, dt_ref, gz_ref, qkv_ref, yconf_ref, dtb_ref, alog_ref, dskip_ref, nw_ref,
                  tri_ref, expand_ref, sink_ref, wo_ref, fw_ref, o_ref,
                  state_ref, kprev_ref, vprev_ref, ys_ref, ya_ref, *, final_norm):
    q = SSD_CHUNK
    w = WINDOW
    k0, v0 = ATTN_WIDTH, ATTN_WIDTH + 2 * LANES
    step = pl.program_id(1)

    @pl.when(step == 0)
    def _():
        state_ref[...] = jnp.zeros(state_ref.shape, F32)
        kprev_ref[...] = jnp.zeros(kprev_ref.shape, BF16)
        vprev_ref[...] = jnp.zeros(vprev_ref.shape, BF16)

    neg_a = -jnp.exp(alog_ref[...])
    nblk = xc_ref.shape[0] // q
    for c in range(nblk):
        rows = slice(c * q, (c + 1) * q)
        ys_ref[rows, :] = _ssd_chunk(xc_ref[rows, :], dt_ref[rows, :], gz_ref[rows, 0:SSD_WIDTH], dtb_ref[...],
                                     neg_a, dskip_ref[...], nw_ref[...], tri_ref[...], expand_ref[...], state_ref)
        if c == 0:
            kprev, vprev = kprev_ref, vprev_ref
            kmin = jnp.where(step > 0, 0, w)
        else:
            prev = slice((c - 1) * w, c * w)
            kprev, vprev = qkv_ref[prev, k0:k0 + 2 * LANES], qkv_ref[prev, v0:v0 + 2 * LANES]
            kmin = 0
        ya_ref[rows, :] = _attn_block(qkv_ref[rows, 0:ATTN_WIDTH], kprev, vprev,
                                      qkv_ref[rows, k0:k0 + 2 * LANES], qkv_ref[rows, v0:v0 + 2 * LANES],
                                      gz_ref[rows, SSD_WIDTH:SSD_WIDTH + ATTN_WIDTH], sink_ref, kmin)
    last = slice((nblk - 1) * w, nblk * w)
    kprev_ref[...] = qkv_ref[last, k0:k0 + 2 * LANES]
    vprev_ref[...] = qkv_ref[last, v0:v0 + 2 * LANES]

    a0 = SSD_WIDTH + ATTN_WIDTH
    acc = (_dot(ys_ref[...], wo_ref[0:SSD_WIDTH, :]) + _dot(ya_ref[...], wo_ref[SSD_WIDTH:a0, :])
           + _dot(yconf_ref[...], wo_ref[a0:, :]))
    xn = x_ref[...] + acc
    if final_norm:
        xn = xn * lax.rsqrt(jnp.mean(xn * xn, axis=-1, keepdims=True) + EPS) * fw_ref[...]
    o_ref[...] = xn


N_PROJ_CONSTS = 9
N_MIX_CONSTS = 9

HEAD_COLS = MIX_WIDTH + SSD_CONV_DIM
QKV_COLS = ATTN_WIDTH + 4 * ATTN_KV_HEADS * ATTN_HEAD_DIM
TAIL_CONF = 0
TAIL_QKV = TAIL_CONF + 2 * CONF_WIDTH
TAIL_DT = TAIL_QKV + QKV_COLS
TAIL_COLS = TAIL_DT + LANES


def _layer_kernel(*refs, final_norm):
    x_ref = refs[0]
    nw_ref, wh_ref, wt_ref, cw_ref, cb_ref, dww_ref, dwb_ref, lnw_ref, lnb_ref = refs[1:1 + N_PROJ_CONSTS]
    mix_consts = refs[1 + N_PROJ_CONSTS:1 + N_PROJ_CONSTS + N_MIX_CONSTS]
    o_ref = refs[1 + N_PROJ_CONSTS + N_MIX_CONSTS]
    (gz_ref, xc_ref, dt_ref, qkv_ref, yconf_ref, cpad_ref, cshift_ref, xpad_ref,
     state_ref, kprev_ref, vprev_ref, ys_ref, ya_ref) = refs[2 + N_PROJ_CONSTS + N_MIX_CONSTS:]
    _in_proj_kernel(x_ref, nw_ref, wh_ref.at[:, 0:MIX_WIDTH], wh_ref.at[:, MIX_WIDTH:HEAD_COLS],
                    wt_ref.at[:, TAIL_DT:TAIL_COLS], wt_ref.at[:, TAIL_QKV:TAIL_DT],
                    wt_ref.at[:, TAIL_CONF:TAIL_QKV],
                    cw_ref, cb_ref, dww_ref, dwb_ref, lnw_ref, lnb_ref,
                    gz_ref, xc_ref, dt_ref, qkv_ref, yconf_ref, cpad_ref, cshift_ref, xpad_ref)
    _mixer_kernel(x_ref, xc_ref, dt_ref, gz_ref, qkv_ref, yconf_ref, *mix_consts, o_ref,
                  state_ref, kprev_ref, vprev_ref, ys_ref, ya_ref, final_norm=final_norm)


def _layer(x, layer, proj_consts, mix_consts, final_norm):
    b, l, d = x.shape
    tm = min(LAYER_ROWS, l)
    assert len(proj_consts) == N_PROJ_CONSTS and len(mix_consts) == N_MIX_CONSTS
    row = pl.BlockSpec((None, tm, d), lambda i, j: (i, j, 0))

    def full(a):
        if a.ndim == 3:
            return pl.BlockSpec((None,) + a.shape[1:], lambda i, j: (layer, 0, 0), pipeline_mode=pl.Buffered(1))
        return pl.BlockSpec(a.shape, lambda i, j: (0, 0), pipeline_mode=pl.Buffered(1))

    qkv_w = QKV_COLS
    return pl.pallas_call(
        functools.partial(_layer_kernel, final_norm=final_norm),
        grid=(b, l // tm),
        in_specs=[row] + [full(a) for a in proj_consts + mix_consts],
        out_specs=row,
        out_shape=jax.ShapeDtypeStruct((b, l, d), F32),
        scratch_shapes=[pltpu.VMEM((tm, SSD_WIDTH + ATTN_WIDTH), BF16),
                        pltpu.VMEM((tm, SSD_CONV_DIM), BF16),
                        pltpu.VMEM((tm, LANES), F32),
                        pltpu.VMEM((tm, qkv_w), BF16),
                        pltpu.VMEM((tm, CONF_WIDTH), BF16),
                        pltpu.VMEM((CONF_HALO + tm, CONF_WIDTH), F32),
                        pltpu.VMEM((SUBLANES - 1, CONF_HALO + tm - SUBLANES, CONF_WIDTH), F32),
                        pltpu.VMEM((SSD_HALO + tm, SSD_CONV_DIM), F32),
                        pltpu.VMEM((SSD_GROUPS, SSD_STATE, SSD_WIDTH // SSD_GROUPS), F32),
                        pltpu.VMEM((WINDOW, 2 * LANES), BF16),
                        pltpu.VMEM((WINDOW, 2 * LANES), BF16),
                        pltpu.VMEM((tm, SSD_WIDTH), BF16),
                        pltpu.VMEM((tm, ATTN_WIDTH), BF16)],
        compiler_params=pltpu.CompilerParams(dimension_semantics=("parallel", "arbitrary"),
                                             vmem_limit_bytes=VMEM_LIMIT),
        name="layer",
    )(x, *proj_consts, *mix_consts)


def _pad_lanes(v, n=LANES):
    return jnp.pad(v.astype(F32), (0, n - v.shape[0]))[None, :]


def kernel(x, norm_w, w_in, ssd_conv_w, ssd_conv_b, ssd_dt_bias, ssd_a_log, ssd_d, ssd_norm_w, attn_sinks, conf_dw_w, conf_dw_b, conf_ln_w, conf_ln_b, w_out, final_norm_w):
    b, l, d = x.shape
    depth = w_in.shape[0]
    kv = ATTN_KV_HEADS * ATTN_HEAD_DIM
    o_xbc = MIX_WIDTH
    o_dt = o_xbc + SSD_CONV_DIM
    o_q = o_dt + SSD_HEADS
    o_k = o_q + ATTN_WIDTH
    o_v = o_k + kv
    o_conf = o_v + kv

    q = SSD_CHUNK
    tri = (lax.broadcasted_iota(jnp.int32, (q, q), 0) >= lax.broadcasted_iota(jnp.int32, (q, q), 1)).astype(BF16)
    head_of_lane = lax.broadcasted_iota(jnp.int32, (LANES, SSD_WIDTH), 1) // SSD_HEAD_DIM
    expand = (lax.broadcasted_iota(jnp.int32, (LANES, SSD_WIDTH), 0) == head_of_lane).astype(BF16)

    def dup_heads(wm):
        wm = wm.reshape(depth, d, ATTN_KV_HEADS, 1, ATTN_HEAD_DIM)
        return jnp.broadcast_to(wm, (depth, d, ATTN_KV_HEADS, 2, ATTN_HEAD_DIM)).reshape(depth, d, 2 * kv)

    w_head = w_in[:, :, :o_dt].astype(BF16)
    dt_pad = jnp.zeros((depth, d, LANES - SSD_HEADS), w_in.dtype)
    w_tail = jnp.concatenate([w_in[:, :, o_conf:], w_in[:, :, o_q:o_k],
                              dup_heads(w_in[:, :, o_k:o_v]), dup_heads(w_in[:, :, o_v:o_conf]),
                              w_in[:, :, o_dt:o_q], dt_pad], axis=-1).astype(BF16)
    assert o_dt == HEAD_COLS and w_tail.shape[-1] == TAIL_COLS
    w_out_b = w_out.astype(BF16)
    for i in range(depth):
        proj_consts = (norm_w[i][None, :], w_head, w_tail, ssd_conv_w[i], ssd_conv_b[i][None, :],
                       conf_dw_w[i], conf_dw_b[i][None, :], conf_ln_w[i][None, :], conf_ln_b[i][None, :])
        sinks = jnp.broadcast_to(attn_sinks[i].astype(F32)[:, None], (ATTN_Q_HEADS, LANES))
        mix_consts = (_pad_lanes(ssd_dt_bias[i]), _pad_lanes(ssd_a_log[i]),
                      jnp.repeat(ssd_d[i], SSD_HEAD_DIM)[None, :], ssd_norm_w[i][None, :], tri, expand, sinks,
                      w_out_b, final_norm_w[None, :])
        x = _layer(x, i, proj_consts, mix_consts, final_norm=(i == depth - 1))
    return x
```

```python
import functools

import jax
import jax.numpy as jnp
from jax import lax
from jax.experimental import pallas as pl
from jax.experimental.pallas import tpu as pltpu

F32 = jnp.float32
BF16 = jnp.bfloat16

D_MODEL = 1024
SSD_WIDTH = 1024
SSD_HEAD_DIM = 64
SSD_HEADS = 16
SSD_GROUPS = 2
SSD_STATE = 128
SSD_CONV = 4
SSD_CHUNK = 128
SSD_CONV_DIM = SSD_WIDTH + 2 * SSD_GROUPS * SSD_STATE
ATTN_HEAD_DIM = 64
ATTN_WIDTH = 512
ATTN_Q_HEADS = 8
ATTN_KV_HEADS = 2
WINDOW = 128
CONF_WIDTH = 512
CONF_KERNEL = 31
MIX_WIDTH = SSD_WIDTH + ATTN_WIDTH + CONF_WIDTH
EPS = 1e-5

LANES = 128
SUBLANES = 8
VMEM_LIMIT = 56 * 1024 * 1024

LAYER_ROWS = 512
CONF_HALO = 32
SSD_HALO = SUBLANES


def _sigmoid(v):
    return 1.0 / (1.0 + jnp.exp(-v))


def _silu(v):
    return v * _sigmoid(v)


def _dot(a, b):
    return jnp.dot(a, b, preferred_element_type=F32)


def _dot_nt(a, b):
    return lax.dot_general(a, b, (((1,), (1,)), ((), ())), preferred_element_type=F32)


def _rows_ahead(v, o):
    return pltpu.roll(v, v.shape[0] - o, axis=0)


def _in_proj_kernel(x_ref, nw_ref, wz_ref, wxbc_ref, wdt_ref, wqkv_ref, wconf_ref,
                    cw_ref, cb_ref, dww_ref, dwb_ref, lnw_ref, lnb_ref,
                    gz_ref, xc_ref, dt_ref, qkv_ref, yconf_ref,
                    cpad_ref, cshift_ref, xpad_ref):
    tm = x_ref.shape[0]

    @pl.when(pl.program_id(1) == 0)
    def _():
        cpad_ref[0:CONF_HALO, :] = jnp.zeros((CONF_HALO, CONF_WIDTH), F32)
        xpad_ref[0:SSD_HALO, :] = jnp.zeros((SSD_HALO, SSD_CONV_DIM), F32)

    x = x_ref[...]
    h = x * lax.rsqrt(jnp.mean(x * x, axis=-1, keepdims=True) + EPS) * nw_ref[...]
    hb = h.astype(BF16)

    c = _dot(hb, wconf_ref[...])
    cpad_ref[CONF_HALO:CONF_HALO + tm, :] = c[:, :CONF_WIDTH] * _sigmoid(c[:, CONF_WIDTH:])
    padded = cpad_ref[...]
    n_shift = CONF_HALO + tm - SUBLANES
    for r in range(1, SUBLANES):
        cshift_ref[r - 1] = _rows_ahead(padded, r)[0:n_shift]
    acc = jnp.broadcast_to(dwb_ref[...], (tm, CONF_WIDTH))
    base = CONF_HALO - (CONF_KERNEL - 1)
    for k in range(CONF_KERNEL):
        a8, r = divmod(base + k, SUBLANES)
        if r == 0:
            tap = cpad_ref[a8 * SUBLANES:a8 * SUBLANES + tm, :]
        else:
            tap = cshift_ref[r - 1, a8 * SUBLANES:a8 * SUBLANES + tm, :]
        acc = acc + dww_ref[k:k + 1, :] * tap
    cpad_ref[0:CONF_HALO, :] = cpad_ref[tm:tm + CONF_HALO, :]
    mu = jnp.mean(acc, axis=-1, keepdims=True)
    cc = acc - mu
    ln = cc * lax.rsqrt(jnp.mean(cc * cc, axis=-1, keepdims=True) + EPS)
    ln = ln * lnw_ref[...] + lnb_ref[...]

    gz = _silu(_dot(hb, wz_ref[...]))
    gz_ref[...] = gz[:, :SSD_WIDTH + ATTN_WIDTH].astype(BF16)
    yconf_ref[...] = (_silu(ln) * gz[:, SSD_WIDTH + ATTN_WIDTH:]).astype(BF16)

    xbc = _dot(hb, wxbc_ref[...])
    xpad_ref[SSD_HALO:SSD_HALO + tm, :] = xbc
    xpadded = xpad_ref[...]
    acc2 = cb_ref[...] + cw_ref[SSD_CONV - 1:SSD_CONV, :] * xbc
    for k in range(SSD_CONV - 1):
        acc2 = acc2 + cw_ref[k:k + 1, :] * _rows_ahead(xpadded, SSD_HALO - (SSD_CONV - 1) + k)[0:tm]
    xpad_ref[0:SSD_HALO, :] = xpad_ref[tm:tm + SSD_HALO, :]
    xc_ref[...] = _silu(acc2).astype(BF16)

    dt_ref[...] = _dot(hb, wdt_ref[...])
    qkv_ref[...] = _dot(hb, wqkv_ref[...]).astype(BF16)


def _attn_block(q, kprev, vprev, kcur, vcur, gz, sink_ref, kmin):
    w = WINDOW
    lane = lax.broadcasted_iota(jnp.int32, (2 * w, LANES), 1)
    lo = lane < ATTN_HEAD_DIM
    qi = lax.broadcasted_iota(jnp.int32, (2 * w, 2 * w), 0) % w
    kk = lax.broadcasted_iota(jnp.int32, (2 * w, 2 * w), 1)
    rel = qi - (kk - w)
    band = (rel >= 0) & (rel < w) & (kk >= kmin)
    row = lax.broadcasted_iota(jnp.int32, (2 * w, 1), 0)
    zero = jnp.zeros((), BF16)
    lo_f = jnp.where(lo, 1.0, 0.0)
    ones_bd = jnp.concatenate([lo_f, 1.0 - lo_f], axis=0).astype(BF16)
    outs = []
    for g in range(ATTN_KV_HEADS):
        kk2 = jnp.concatenate([kprev[:, g * LANES:(g + 1) * LANES], kcur[:, g * LANES:(g + 1) * LANES]], axis=0)
        vv2 = jnp.concatenate([vprev[:, g * LANES:(g + 1) * LANES], vcur[:, g * LANES:(g + 1) * LANES]], axis=0)
        kbd = jnp.concatenate([jnp.where(lo, kk2, zero), jnp.where(lo, zero, kk2)], axis=0)
        vbd = jnp.concatenate([jnp.where(lo, vv2, zero), jnp.where(lo, zero, vv2)], axis=0)
        qg = q[:, g * 2 * LANES:(g + 1) * 2 * LANES]
        q2 = jnp.concatenate([qg[:, :LANES], qg[:, LANES:]], axis=0)
        s = _dot_nt(q2, kbd) * (ATTN_HEAD_DIM ** -0.5)
        ps, sink_terms = [], []
        for half in range(2):
            sh = jnp.where(band, s[:, half * 2 * w:(half + 1) * 2 * w], -jnp.inf)
            h0 = 4 * g + half
            sk = jnp.where(row < w, sink_ref[h0:h0 + 1, 0:1], sink_ref[h0 + 2:h0 + 3, 0:1])
            m = jnp.maximum(jnp.max(sh, axis=-1, keepdims=True), sk)
            ps.append(jnp.exp(sh - m).astype(BF16))
            sink_terms.append(jnp.exp(sk - m))
        o2 = _dot(jnp.concatenate(ps, axis=1), jnp.concatenate([vbd, ones_bd], axis=1))
        den = o2[:, LANES:] + jnp.where(lo, sink_terms[0], sink_terms[1])
        o2 = o2[:, :LANES] / den
        outs.append(o2[:w])
        outs.append(o2[w:])
    o = jnp.concatenate(outs, axis=1)
    return (o * gz.astype(F32)).astype(BF16)


def _ssd_chunk(xc, dt, gz, dtb, neg_a, dskip, nw, tri, expand, state_ref):
    q = SSD_CHUNK
    gw = SSD_WIDTH // SSD_GROUPS
    xs_b = xc[:, :SSD_WIDTH]
    xs = xs_b.astype(F32)

    dtv = dt + dtb
    dtp = jnp.maximum(dtv, 0.0) + jnp.log1p(jnp.exp(-jnp.abs(dtv)))
    a = dtp * neg_a
    a_hi = a.astype(BF16)
    r1 = a - a_hi.astype(F32)
    a_mid = r1.astype(BF16)
    a_lo = (r1 - a_mid.astype(F32)).astype(BF16)
    acs = _dot(tri, a_hi) + _dot(tri, a_mid) + _dot(tri, a_lo)
    acs_t = acs.T
    dtp_t = dtp.T
    decay_in = jnp.exp(acs)
    w_end = dtp * jnp.exp(acs[q - 1:q, :] - acs)
    w_end_x = _dot(w_end.astype(BF16), expand)
    decay_in_x = _dot(decay_in.astype(BF16), expand)
    xdd = (xs * w_end_x).astype(BF16)

    ii = lax.broadcasted_iota(jnp.int32, (q, q), 0)
    jj = lax.broadcasted_iota(jnp.int32, (q, q), 1)
    causal = ii >= jj
    lane = lax.broadcasted_iota(jnp.int32, (q, LANES), 1)
    lo = lane < SSD_HEAD_DIM
    zero = jnp.zeros((), BF16)
    heads_per_group = SSD_HEADS // SSD_GROUPS
    y_parts = []
    for g in range(SSD_GROUPS):
        bg = xc[:, SSD_WIDTH + g * SSD_STATE:SSD_WIDTH + (g + 1) * SSD_STATE]
        cg = xc[:, SSD_WIDTH + (SSD_GROUPS + g) * SSD_STATE:SSD_WIDTH + (SSD_GROUPS + g + 1) * SSD_STATE]
        cbm = _dot_nt(cg, bg)
        sg = state_ref[g]
        y_off = _dot(cg, sg.astype(BF16))
        upd = lax.dot_general(bg, xdd[:, g * gw:(g + 1) * gw], (((0,), (0,)), ((), ())),
                              preferred_element_type=F32)
        state_ref[g] = decay_in_x[q - 1:q, g * gw:(g + 1) * gw] * sg + upd
        for p in range(heads_per_group // 2):
            ms = []
            for hh in range(2):
                h = g * heads_per_group + 2 * p + hh
                seg = acs[:, h:h + 1] - acs_t[h:h + 1, :]
                lmat = jnp.exp(jnp.where(causal, seg, -jnp.inf))
                ms.append((cbm * lmat * dtp_t[h:h + 1, :]).astype(BF16))
            c0 = g * gw + p * LANES
            xp = xs_b[:, c0:c0 + LANES]
            rhs = jnp.concatenate([jnp.where(lo, xp, zero), jnp.where(lo, zero, xp)], axis=0)
            y_diag = _dot(jnp.concatenate(ms, axis=1), rhs)
            y_parts.append(y_diag + y_off[:, p * LANES:(p + 1) * LANES] * decay_in_x[:, c0:c0 + LANES])
    y = jnp.concatenate(y_parts, axis=1) + dskip * xs

    gated = y * gz.astype(F32)
    outs = []
    for g in range(SSD_GROUPS):
        gg = gated[:, g * gw:(g + 1) * gw]
        outs.append(gg * lax.rsqrt(jnp.mean(gg * gg, axis=-1, keepdims=True) + EPS))
    return (jnp.concatenate(outs, axis=1) * nw).astype(BF16)


def _mixer_kernel(x_ref, xc_ref, dt_ref, gz_ref, qkv_ref, yconf_ref, dtb_ref, alog_ref, dskip_ref, nw_ref,
                  tri_ref, expand_ref, sink_ref, wo_ref, fw_ref, o_ref,
                  state_ref, kprev_ref, vprev_ref, ys_ref, ya_ref, *, final_norm):
    q = SSD_CHUNK
    w = WINDOW
    k0, v0 = ATTN_WIDTH, ATTN_WIDTH + 2 * LANES
    step = pl.program_id(1)

    @pl.when(step == 0)
    def _():
        state_ref[...] = jnp.zeros(state_ref.shape, F32)
        kprev_ref[...] = jnp.zeros(kprev_ref.shape, BF16)
        vprev_ref[...] = jnp.zeros(vprev_ref.shape, BF16)

    neg_a = -jnp.exp(alog_ref[...])
    nblk = xc_ref.shape[0] // q
    def chunk(c, carry):
        rows = pl.ds(pl.multiple_of(c * q, q), q)
        ys_ref[rows, :] = _ssd_chunk(xc_ref[rows, :], dt_ref[rows, :], gz_ref[rows, 0:SSD_WIDTH], dtb_ref[...],
                                     neg_a, dskip_ref[...], nw_ref[...], tri_ref[...], expand_ref[...], state_ref)
        kmin = jnp.where(step + c > 0, 0, w)
        kcur, vcur = qkv_ref[rows, k0:k0 + 2 * LANES], qkv_ref[rows, v0:v0 + 2 * LANES]
        ya_ref[rows, :] = _attn_block(qkv_ref[rows, 0:ATTN_WIDTH], kprev_ref, vprev_ref, kcur, vcur,
                                      gz_ref[rows, SSD_WIDTH:SSD_WIDTH + ATTN_WIDTH], sink_ref, kmin)
        kprev_ref[...] = kcur
        vprev_ref[...] = vcur
        return carry

    lax.fori_loop(0, nblk, chunk, 0)

    a0 = SSD_WIDTH + ATTN_WIDTH
    acc = (_dot(ys_ref[...], wo_ref[0:SSD_WIDTH, :]) + _dot(ya_ref[...], wo_ref[SSD_WIDTH:a0, :])
           + _dot(yconf_ref[...], wo_ref[a0:, :]))
    xn = x_ref[...] + acc
    if final_norm:
        xn = xn * lax.rsqrt(jnp.mean(xn * xn, axis=-1, keepdims=True) + EPS) * fw_ref[...]
    o_ref[...] = xn


N_PROJ_CONSTS = 9
N_MIX_CONSTS = 9

HEAD_COLS = MIX_WIDTH + SSD_CONV_DIM
QKV_COLS = ATTN_WIDTH + 4 * ATTN_KV_HEADS * ATTN_HEAD_DIM
TAIL_CONF = 0
TAIL_QKV = TAIL_CONF + 2 * CONF_WIDTH
TAIL_DT = TAIL_QKV + QKV_COLS
TAIL_COLS = TAIL_DT + LANES


def _layer_kernel(*refs, final_norm):
    x_ref = refs[0]
    nw_ref, wh_ref, wt_ref, cw_ref, cb_ref, dww_ref, dwb_ref, lnw_ref, lnb_ref = refs[1:1 + N_PROJ_CONSTS]
    mix_consts = refs[1 + N_PROJ_CONSTS:1 + N_PROJ_CONSTS + N_MIX_CONSTS]
    o_ref = refs[1 + N_PROJ_CONSTS + N_MIX_CONSTS]
    (gz_ref, xc_ref, dt_ref, qkv_ref, yconf_ref, cpad_ref, cshift_ref, xpad_ref,
     state_ref, kprev_ref, vprev_ref, ys_ref, ya_ref) = refs[2 + N_PROJ_CONSTS + N_MIX_CONSTS:]
    _in_proj_kernel(x_ref, nw_ref, wh_ref.at[:, 0:MIX_WIDTH], wh_ref.at[:, MIX_WIDTH:HEAD_COLS],
                    wt_ref.at[:, TAIL_DT:TAIL_COLS], wt_ref.at[:, TAIL_QKV:TAIL_DT],
                    wt_ref.at[:, TAIL_CONF:TAIL_QKV],
                    cw_ref, cb_ref, dww_ref, dwb_ref, lnw_ref, lnb_ref,
                    gz_ref, xc_ref, dt_ref, qkv_ref, yconf_ref, cpad_ref, cshift_ref, xpad_ref)
    _mixer_kernel(x_ref, xc_ref, dt_ref, gz_ref, qkv_ref, yconf_ref, *mix_consts, o_ref,
                  state_ref, kprev_ref, vprev_ref, ys_ref, ya_ref, final_norm=final_norm)


def _layer(x, layer, proj_consts, mix_consts, final_norm):
    b, l, d = x.shape
    tm = min(LAYER_ROWS, l)
    assert len(proj_consts) == N_PROJ_CONSTS and len(mix_consts) == N_MIX_CONSTS
    row = pl.BlockSpec((None, tm, d), lambda i, j: (i, j, 0))

    def full(a):
        if a.ndim == 3:
            cols = HEAD_COLS if a is proj_consts[1] else a.shape[2]
            return pl.BlockSpec((None, a.shape[1], cols), lambda i, j: (layer, 0, 0),
                                pipeline_mode=pl.Buffered(1))
        return pl.BlockSpec(a.shape, lambda i, j: (0, 0), pipeline_mode=pl.Buffered(1))

    qkv_w = QKV_COLS
    return pl.pallas_call(
        functools.partial(_layer_kernel, final_norm=final_norm),
        grid=(b, l // tm),
        in_specs=[row] + [full(a) for a in proj_consts + mix_consts],
        out_specs=row,
        out_shape=jax.ShapeDtypeStruct((b, l, d), F32),
        scratch_shapes=[pltpu.VMEM((tm, SSD_WIDTH + ATTN_WIDTH), BF16),
                        pltpu.VMEM((tm, SSD_CONV_DIM), BF16),
                        pltpu.VMEM((tm, LANES), F32),
                        pltpu.VMEM((tm, qkv_w), BF16),
                        pltpu.VMEM((tm, CONF_WIDTH), BF16),
                        pltpu.VMEM((CONF_HALO + tm, CONF_WIDTH), F32),
                        pltpu.VMEM((SUBLANES - 1, CONF_HALO + tm - SUBLANES, CONF_WIDTH), F32),
                        pltpu.VMEM((SSD_HALO + tm, SSD_CONV_DIM), F32),
                        pltpu.VMEM((SSD_GROUPS, SSD_STATE, SSD_WIDTH // SSD_GROUPS), F32),
                        pltpu.VMEM((WINDOW, 2 * LANES), BF16),
                        pltpu.VMEM((WINDOW, 2 * LANES), BF16),
                        pltpu.VMEM((tm, SSD_WIDTH), BF16),
                        pltpu.VMEM((tm, ATTN_WIDTH), BF16)],
        compiler_params=pltpu.CompilerParams(dimension_semantics=("parallel", "arbitrary"),
                                             vmem_limit_bytes=VMEM_LIMIT),
        name="layer",
    )(x, *proj_consts, *mix_consts)


def _pad_lanes(v, n=LANES):
    return jnp.pad(v.astype(F32), (0, n - v.shape[0]))[None, :]


def kernel(x, norm_w, w_in, ssd_conv_w, ssd_conv_b, ssd_dt_bias, ssd_a_log, ssd_d, ssd_norm_w, attn_sinks, conf_dw_w, conf_dw_b, conf_ln_w, conf_ln_b, w_out, final_norm_w):
    b, l, d = x.shape
    depth = w_in.shape[0]
    kv = ATTN_KV_HEADS * ATTN_HEAD_DIM
    o_xbc = MIX_WIDTH
    o_dt = o_xbc + SSD_CONV_DIM
    o_q = o_dt + SSD_HEADS
    o_k = o_q + ATTN_WIDTH
    o_v = o_k + kv
    o_conf = o_v + kv

    q = SSD_CHUNK
    tri = (lax.broadcasted_iota(jnp.int32, (q, q), 0) >= lax.broadcasted_iota(jnp.int32, (q, q), 1)).astype(BF16)
    head_of_lane = lax.broadcasted_iota(jnp.int32, (LANES, SSD_WIDTH), 1) // SSD_HEAD_DIM
    expand = (lax.broadcasted_iota(jnp.int32, (LANES, SSD_WIDTH), 0) == head_of_lane).astype(BF16)

    def dup_heads(wm):
        wm = wm.reshape(depth, d, ATTN_KV_HEADS, 1, ATTN_HEAD_DIM)
        return jnp.broadcast_to(wm, (depth, d, ATTN_KV_HEADS, 2, ATTN_HEAD_DIM)).reshape(depth, d, 2 * kv)

    w_in_b = w_in.astype(BF16)
    dt_pad = jnp.zeros((depth, d, LANES - SSD_HEADS), BF16)
    w_tail = jnp.concatenate([w_in_b[:, :, o_conf:], w_in_b[:, :, o_q:o_k],
                              dup_heads(w_in_b[:, :, o_k:o_v]), dup_heads(w_in_b[:, :, o_v:o_conf]),
                              w_in_b[:, :, o_dt:o_q], dt_pad], axis=-1)
    assert o_dt == HEAD_COLS and w_tail.shape[-1] == TAIL_COLS
    w_out_b = w_out.astype(BF16)
    for i in range(depth):
        proj_consts = (norm_w[i][None, :], w_in_b, w_tail, ssd_conv_w[i], ssd_conv_b[i][None, :],
                       conf_dw_w[i], conf_dw_b[i][None, :], conf_ln_w[i][None, :], conf_ln_b[i][None, :])
        sinks = jnp.broadcast_to(attn_sinks[i].astype(F32)[:, None], (ATTN_Q_HEADS, LANES))
        mix_consts = (_pad_lanes(ssd_dt_bias[i]), _pad_lanes(ssd_a_log[i]),
                      jnp.repeat(ssd_d[i], SSD_HEAD_DIM)[None, :], ssd_norm_w[i][None, :], tri, expand, sinks,
                      w_out_b, final_norm_w[None, :])
        x = _layer(x, i, proj_consts, mix_consts, final_norm=(i == depth - 1))
    return x
```
